```python
import math
import jax, jax.numpy as jnp
from jax import lax
import numpy as np

D_MODEL = 1024
BATCH = 2
SEQ = 8192
DEPTH = 1

D_PLE = 256
D_MIX = D_MODEL
D_CONV = D_MIX // 2
D_SGU = D_MIX - D_CONV
CONV_GROUPS = 8
CONV_WIDTH = 31
N_SGU_HEADS = 8
SGU_HEAD_DIM = D_SGU // N_SGU_HEADS
CHUNK = 128
LN_EPS = 1e-5
ALPHA = (2 * DEPTH) ** 0.25
BETA = (8 * DEPTH) ** -0.25
D_IN = 3 * D_CONV + 3 * D_SGU

kernel_name = "hybrid_conformer_conv_chunked_sgu_deepnorm"


def _layer_norm(x, g, b):
    xf = x.astype(jnp.float32)
    mu = jnp.mean(xf, axis=-1, keepdims=True)
    var = jnp.mean(jnp.square(xf - mu), axis=-1, keepdims=True)
    y = (xf - mu) * lax.rsqrt(var + LN_EPS)
    return (y * g.astype(jnp.float32) + b.astype(jnp.float32)).astype(x.dtype)


def _causal_depthwise_conv(a, w, b):
    out = lax.conv_general_dilated(
        a, w[:, None, :].astype(a.dtype),
        window_strides=(1,),
        padding=[(CONV_WIDTH - 1, 0)],
        dimension_numbers=("NWC", "WIO", "NWC"),
        feature_group_count=a.shape[-1])
    return out + b


def _conformer_conv_branch(a_val, a_gate, a_z, conv_w, conv_b, ln_g, ln_b):
    a = a_val * jax.nn.sigmoid(a_gate)
    a = _causal_depthwise_conv(a, conv_w, conv_b)
    a = jax.nn.silu(_layer_norm(a, ln_g, ln_b))
    return a * jax.nn.silu(a_z)


def _chunked_sgu_branch(b_u, b_v, b_z, ln_g, ln_b, w_s, b_s):
    bsz, seq, _ = b_u.shape
    u = jax.nn.gelu(b_u, approximate=False)
    v = _layer_norm(jax.nn.gelu(b_v, approximate=False), ln_g, ln_b)
    vh = v.reshape(bsz, seq // CHUNK, CHUNK, N_SGU_HEADS, SGU_HEAD_DIM)
    mask = jnp.tril(jnp.ones((CHUNK, CHUNK), dtype=bool))
    w = jnp.where(mask[None], w_s, jnp.zeros((), w_s.dtype))
    sv = jnp.einsum("hts,bcshd->bcthd", w, vh)
    sv = sv + jnp.transpose(b_s)[None, None, :, :, None]
    sv = sv.reshape(bsz, seq, D_SGU)
    return (u * sv) * jax.nn.silu(b_z)


def setup_inputs(seed: int = 0) -> dict:
    key = jax.random.key(seed)
    ks = jax.random.split(key, 20)
    f32 = jnp.float32
    nrm = lambda k, shape, s: jax.random.normal(k, shape, f32) * s
    return {
        "x": nrm(ks[0], (BATCH, SEQ, D_MODEL), 1.0),
        "p": nrm(ks[1], (DEPTH, BATCH, SEQ, D_PLE), 1.0),
        "ln_emb_g": 1.0 + nrm(ks[2], (D_MODEL,), 0.02),
        "ln_emb_b": nrm(ks[3], (D_MODEL,), 0.02),
        "w_in": nrm(ks[4], (DEPTH, D_MODEL, D_IN), D_MODEL ** -0.5),
        "conv_w": nrm(ks[5], (DEPTH, CONV_WIDTH, D_CONV), CONV_WIDTH ** -0.5),
        "conv_b": nrm(ks[6], (DEPTH, D_CONV), 0.02),
        "conv_ln_g": 1.0 + nrm(ks[7], (DEPTH, D_CONV), 0.02),
        "conv_ln_b": nrm(ks[8], (DEPTH, D_CONV), 0.02),
        "sgu_ln_g": 1.0 + nrm(ks[9], (DEPTH, D_SGU), 0.02),
        "sgu_ln_b": nrm(ks[10], (DEPTH, D_SGU), 0.02),
        "w_s": nrm(ks[11], (DEPTH, N_SGU_HEADS, CHUNK, CHUNK), CHUNK ** -0.5),
        "b_s": 1.0 + nrm(ks[12], (DEPTH, N_SGU_HEADS, CHUNK), 0.02),
        "w_out": nrm(ks[13], (DEPTH, D_MIX, D_MODEL), BETA * D_MIX ** -0.5),
        "post_ln_g": 1.0 + nrm(ks[14], (DEPTH, D_MODEL), 0.02),
        "post_ln_b": nrm(ks[15], (DEPTH, D_MODEL), 0.02),
        "w_ple": nrm(ks[16], (DEPTH, D_PLE, D_MODEL), D_PLE ** -0.5),
        "w_ple_gate": nrm(ks[17], (DEPTH, D_MODEL, D_MODEL), D_MODEL ** -0.5),
        "b_ple_gate": nrm(ks[18], (DEPTH, D_MODEL), 0.02),
    }


def reference(x, p, ln_emb_g, ln_emb_b, w_in, conv_w, conv_b, conv_ln_g, conv_ln_b,
              sgu_ln_g, sgu_ln_b, w_s, b_s, w_out, post_ln_g, post_ln_b,
              w_ple, w_ple_gate, b_ple_gate):
    splits = [D_CONV, 2 * D_CONV, 3 * D_CONV, 3 * D_CONV + D_SGU, 3 * D_CONV + 2 * D_SGU]
    h = _layer_norm(x, ln_emb_g, ln_emb_b)
    for i in range(DEPTH):
        proj = jnp.einsum("bsd,de->bse", h, w_in[i])
        a_val, a_gate, a_z, b_u, b_v, b_z = jnp.split(proj, splits, axis=-1)
        y_a = _conformer_conv_branch(a_val, a_gate, a_z, conv_w[i], conv_b[i],
                                     conv_ln_g[i], conv_ln_b[i])
        y_b = _chunked_sgu_branch(b_u, b_v, b_z, sgu_ln_g[i], sgu_ln_b[i], w_s[i], b_s[i])
        y = jnp.concatenate([y_a, y_b], axis=-1)
        mix = jnp.einsum("bse,ed->bsd", y, w_out[i])
        h = _layer_norm(ALPHA * h + mix, post_ln_g[i], post_ln_b[i])
        pe = jnp.einsum("bsk,kd->bsd", p[i], w_ple[i])
        gate = jax.nn.sigmoid(jnp.einsum("bsd,de->bse", h, w_ple_gate[i]) + b_ple_gate[i])
        h = h + gate * pe
    return h
```

```python
import functools

import jax
import jax.numpy as jnp
from jax import lax
from jax.experimental import pallas as pl
from jax.experimental.pallas import tpu as pltpu

D_MODEL = 1024
D_PLE = 256
D_CONV = 512
D_SGU = 512
CONV_WIDTH = 31
N_SGU_HEADS = 8
SGU_HEAD_DIM = 64
CHUNK = 128
LN_EPS = 1e-5
DEPTH = 1
ALPHA = (2 * DEPTH) ** 0.25

TILE = 512
ROWS = 64
HALO = 32
N_PAIRS = N_SGU_HEADS // 2
VMEM_LIMIT_BYTES = 56 * 1024 * 1024

_INV_SQRT2 = 0.7071067811865476


def _ln(x, g, b):
    mu = jnp.mean(x, axis=-1, keepdims=True)
    xc = x - mu
    var = jnp.mean(xc * xc, axis=-1, keepdims=True)
    return xc * lax.rsqrt(var + LN_EPS) * g + b


def _sigmoid(x):
    return 1.0 / (1.0 + jnp.exp(-x))


def _silu(x):
    return x * _sigmoid(x)


def _gelu(x):
    return 0.5 * x * (1.0 + lax.erf(x * _INV_SQRT2))


def _layer_kernel(x_ref, p_ref, ln_g_ref, ln_b_ref, w_in_ref, conv_w_ref, conv_b_ref,
                  cln_g_ref, cln_b_ref, sln_g_ref, sln_b_ref, ws_ref, bs_ref, w_out_ref,
                  pln_g_ref, pln_b_ref, w_ple_ref, w_gate_ref, b_gate_ref,
                  out_ref,
                  h_scr, hb_scr, a_scr, za_scr, u_scr, vb_scr, zb_scr, y_scr, wsm_scr):
    s = pl.program_id(1)
    bf16 = jnp.bfloat16
    f32 = jnp.float32

    @pl.when((pl.program_id(0) == 0) & (s == 0))
    def _():
        row = lax.broadcasted_iota(jnp.int32, (CHUNK, 2 * CHUNK), 0)
        col = lax.broadcasted_iota(jnp.int32, (CHUNK, 2 * CHUNK), 1)
        keep = (col & (CHUNK - 1)) <= row
        for j in range(N_PAIRS):
            wsm_scr[j] = jnp.where(keep, ws_ref[j], 0.0).astype(bf16)

    @pl.when(s == 0)
    def _():
        a_scr[0:HALO, :] = jnp.zeros((HALO, D_CONV), f32)

    for r in range(0, TILE, ROWS):
        h = _ln(x_ref[0, r:r + ROWS, :], ln_g_ref[...], ln_b_ref[...])
        h_scr[r:r + ROWS, :] = h
        hb_scr[r:r + ROWS, :] = h.astype(bf16)

    def proj(g):
        return jnp.dot(hb_scr[...], w_in_ref[:, g * 512:(g + 1) * 512],
                       preferred_element_type=f32)

    a_val = proj(0)
    a_gate = proj(1)
    a_scr[HALO:HALO + TILE, :] = a_val * _sigmoid(a_gate)
    za_scr[...] = _silu(proj(2))
    u_scr[...] = _gelu(proj(3))
    vb_scr[...] = _ln(_gelu(proj(4)), sln_g_ref[...], sln_b_ref[...]).astype(bf16)
    zb_scr[...] = _silu(proj(5))

    for r in range(0, TILE, ROWS):
        acc = jnp.broadcast_to(conv_b_ref[...], (ROWS, D_CONV))
        for k in range(CONV_WIDTH):
            off = r + HALO - (CONV_WIDTH - 1) + k
            acc = acc + conv_w_ref[k:k + 1, :] * a_scr[off:off + ROWS, :]
        ya = _silu(_ln(acc, cln_g_ref[...], cln_b_ref[...])) * za_scr[r:r + ROWS, :]
        y_scr[r:r + ROWS, 0:D_CONV] = ya.astype(bf16)
    a_scr[0:HALO, :] = a_scr[TILE:TILE + HALO, :]

    lane = lax.broadcasted_iota(jnp.int32, (CHUNK, 2 * SGU_HEAD_DIM), 1)
    lo = lane < SGU_HEAD_DIM
    for c in range(0, TILE, CHUNK):
        for j in range(N_PAIRS):
            cols = slice(j * 2 * SGU_HEAD_DIM, (j + 1) * 2 * SGU_HEAD_DIM)
            vp = vb_scr[c:c + CHUNK, cols]
            zero = jnp.zeros_like(vp)
            rhs = jnp.concatenate([jnp.where(lo, vp, zero), jnp.where(lo, zero, vp)], axis=0)
            sv = jnp.dot(wsm_scr[j], rhs, preferred_element_type=f32) + bs_ref[:, cols]
            yb = u_scr[c:c + CHUNK, cols] * sv * zb_scr[c:c + CHUNK, cols]
            y_scr[c:c + CHUNK, D_CONV + j * 128:D_CONV + (j + 1) * 128] = yb.astype(bf16)

    mix = jnp.dot(y_scr[...], w_out_ref[...], preferred_element_type=f32)
    h1 = _ln(ALPHA * h_scr[...] + mix, pln_g_ref[...], pln_b_ref[...])
    h_scr[...] = h1
    hb_scr[...] = h1.astype(bf16)

    pe = jnp.dot(p_ref[0].astype(bf16), w_ple_ref[...], preferred_element_type=f32)
    gate = _sigmoid(jnp.dot(hb_scr[...], w_gate_ref[...], preferred_element_type=f32)
                    + b_gate_ref[...])
    out_ref[0] = h_scr[...] + gate * pe


def kernel(x, p, ln_emb_g, ln_emb_b, w_in, conv_w, conv_b, conv_ln_g, conv_ln_b, sgu_ln_g, sgu_ln_b, w_s, b_s, w_out, post_ln_g, post_ln_b, w_ple, w_ple_gate, b_ple_gate):
    batch, seq, d_model = x.shape
    assert d_model == D_MODEL and seq % TILE == 0 and p.shape[0] == DEPTH
    bf16 = jnp.bfloat16
    f32 = jnp.float32
    row = lambda v: v.reshape(1, -1).astype(f32)

    ws_pairs = w_s[0].reshape(N_PAIRS, 2, CHUNK, CHUNK).transpose(0, 2, 1, 3).reshape(
        N_PAIRS, CHUNK, 2 * CHUNK)
    bs_full = jnp.repeat(b_s[0].T, SGU_HEAD_DIM, axis=1)

    const = lambda shape: pl.BlockSpec(shape, lambda b, s: (0,) * len(shape))
    in_specs = [
        pl.BlockSpec((1, TILE, D_MODEL), lambda b, s: (b, s, 0)),
        pl.BlockSpec((1, TILE, D_PLE), lambda b, s: (b, s, 0)),
        const((1, D_MODEL)), const((1, D_MODEL)),
        const((D_MODEL, 3 * D_CONV + 3 * D_SGU)),
        const((CONV_WIDTH, D_CONV)), const((1, D_CONV)),
        const((1, D_CONV)), const((1, D_CONV)),
        const((1, D_SGU)), const((1, D_SGU)),
        const((N_PAIRS, CHUNK, 2 * CHUNK)), const((CHUNK, D_SGU)),
        const((D_MODEL, D_MODEL)),
        const((1, D_MODEL)), const((1, D_MODEL)),
        const((D_PLE, D_MODEL)), const((D_MODEL, D_MODEL)),
        const((1, D_MODEL)),
    ]
    scratch = [
        pltpu.VMEM((TILE, D_MODEL), f32),
        pltpu.VMEM((TILE, D_MODEL), bf16),
        pltpu.VMEM((HALO + TILE, D_CONV), f32),
        pltpu.VMEM((TILE, D_CONV), f32),
        pltpu.VMEM((TILE, D_SGU), f32),
        pltpu.VMEM((TILE, D_SGU), bf16),
        pltpu.VMEM((TILE, D_SGU), f32),
        pltpu.VMEM((TILE, D_MODEL), bf16),
        pltpu.VMEM((N_PAIRS, CHUNK, 2 * CHUNK), bf16),
    ]
    return pl.pallas_call(
        _layer_kernel,
        grid=(batch, seq // TILE),
        in_specs=in_specs,
        out_specs=pl.BlockSpec((1, TILE, D_MODEL), lambda b, s: (b, s, 0)),
        out_shape=jax.ShapeDtypeStruct(x.shape, x.dtype),
        scratch_shapes=scratch,
        compiler_params=pltpu.CompilerParams(
            dimension_semantics=("arbitrary", "arbitrary"),
            vmem_limit_bytes=VMEM_LIMIT_BYTES),
        name="deepnorm_hybrid_layer",
    )(x, p[0], row(ln_emb_g), row(ln_emb_b), w_in[0].astype(bf16),
      conv_w[0].astype(f32), row(conv_b[0]), row(conv_ln_g[0]), row(conv_ln_b[0]),
      row(sgu_ln_g[0]), row(sgu_ln_b[0]), ws_pairs, bs_full, w_out[0].astype(bf16),
      row(post_ln_g[0]), row(post_ln_b[0]), w_ple[0].astype(bf16),
      w_ple_gate[0].astype(bf16), row(b_ple_gate[0]))
```

```python
import jax
import jax.numpy as jnp
from jax import lax
from jax.experimental import pallas as pl
from jax.experimental.pallas import tpu as pltpu

D_MODEL = 1024
D_PLE = 256
D_CONV = 512
D_SGU = 512
CONV_WIDTH = 31
N_SGU_HEADS = 8
SGU_HEAD_DIM = 64
CHUNK = 128
LN_EPS = 1e-5
DEPTH = 1
ALPHA = (2 * DEPTH) ** 0.25

LANES = 128
TILE = 512
ROWS = 64
HALO = 32
N_PAIRS = N_SGU_HEADS // 2
N_CONV_SLABS = D_CONV // LANES
VMEM_LIMIT_BYTES = 56 * 1024 * 1024

_INV_SQRT2 = 0.7071067811865476


def _ln(x, g, b):
    mu = jnp.mean(x, axis=-1, keepdims=True)
    xc = x - mu
    var = jnp.mean(xc * xc, axis=-1, keepdims=True)
    return xc * lax.rsqrt(var + LN_EPS) * g + b


def _sigmoid(x):
    return 1.0 / (1.0 + jnp.exp(-x))


def _silu(x):
    return x * _sigmoid(x)


def _gelu(x):
    return 0.5 * x * (1.0 + lax.erf(x * _INV_SQRT2))


def _layer_kernel(x_ref, p_ref, ln_g_ref, ln_b_ref, w_in_ref, conv_w_ref, conv_b_ref,
                  cln_g_ref, cln_b_ref, sln_g_ref, sln_b_ref, ws_ref, bs_ref, w_out_ref,
                  pln_g_ref, pln_b_ref, w_ple_ref, w_gate_ref, b_gate_ref,
                  out_ref,
                  h_scr, hb_scr, a_scr, za_scr, u_scr, vb_scr, zb_scr, y_scr, wsm_scr):
    s = pl.program_id(1)
    bf16 = jnp.bfloat16
    f32 = jnp.float32

    @pl.when((pl.program_id(0) == 0) & (s == 0))
    def _():
        row = lax.broadcasted_iota(jnp.int32, (CHUNK, 2 * CHUNK), 0)
        col = lax.broadcasted_iota(jnp.int32, (CHUNK, 2 * CHUNK), 1)
        keep = (col & (CHUNK - 1)) <= row
        for j in range(N_PAIRS):
            wsm_scr[j] = jnp.where(keep, ws_ref[j], 0.0).astype(bf16)

    @pl.when(s == 0)
    def _():
        a_scr[:, 0:HALO, :] = jnp.zeros((N_CONV_SLABS, HALO, LANES), f32)

    for r in range(0, TILE, ROWS):
        h = _ln(x_ref[0, r:r + ROWS, :], ln_g_ref[...], ln_b_ref[...])
        h_scr[r:r + ROWS, :] = h
        hb_scr[r:r + ROWS, :] = h.astype(bf16)

    def proj(g):
        return jnp.dot(hb_scr[...], w_in_ref[:, g * 512:(g + 1) * 512],
                       preferred_element_type=f32)

    a = proj(0) * _sigmoid(proj(1))
    for c in range(N_CONV_SLABS):
        a_scr[c, HALO:HALO + TILE, :] = a[:, c * LANES:(c + 1) * LANES]
    za_scr[...] = _silu(proj(2))
    u_scr[...] = _gelu(proj(3))
    vb_scr[...] = _ln(_gelu(proj(4)), sln_g_ref[...], sln_b_ref[...]).astype(bf16)
    zb_scr[...] = _silu(proj(5))

    for r in range(0, TILE, ROWS):
        slabs = []
        for c in range(N_CONV_SLABS):
            cols = slice(c * LANES, (c + 1) * LANES)
            acc = jnp.broadcast_to(conv_b_ref[:, cols], (ROWS, LANES))
            for k in range(CONV_WIDTH):
                off = r + HALO - (CONV_WIDTH - 1) + k
                acc = acc + conv_w_ref[k:k + 1, cols] * a_scr[c, off:off + ROWS, :]
            slabs.append(acc)
        conv = jnp.concatenate(slabs, axis=-1)
        ya = _silu(_ln(conv, cln_g_ref[...], cln_b_ref[...])) * za_scr[r:r + ROWS, :]
        y_scr[r:r + ROWS, 0:D_CONV] = ya.astype(bf16)
    a_scr[:, 0:HALO, :] = a_scr[:, TILE:TILE + HALO, :]

    lane = lax.broadcasted_iota(jnp.int32, (CHUNK, 2 * SGU_HEAD_DIM), 1)
    lo = lane < SGU_HEAD_DIM
    for c in range(0, TILE, CHUNK):
        for j in range(N_PAIRS):
            cols = slice(j * 2 * SGU_HEAD_DIM, (j + 1) * 2 * SGU_HEAD_DIM)
            vp = vb_scr[c:c + CHUNK, cols]
            zero = jnp.zeros_like(vp)
            rhs = jnp.concatenate([jnp.where(lo, vp, zero), jnp.where(lo, zero, vp)], axis=0)
            sv = jnp.dot(wsm_scr[j], rhs, preferred_element_type=f32) + bs_ref[:, cols]
            yb = u_scr[c:c + CHUNK, cols] * sv * zb_scr[c:c + CHUNK, cols]
            y_scr[c:c + CHUNK, D_CONV + j * 128:D_CONV + (j + 1) * 128] = yb.astype(bf16)

    mix = jnp.dot(y_scr[...], w_out_ref[...], preferred_element_type=f32)
    h1 = _ln(ALPHA * h_scr[...] + mix, pln_g_ref[...], pln_b_ref[...])
    h_scr[...] = h1
    hb_scr[...] = h1.astype(bf16)

    pe = jnp.dot(p_ref[0].astype(bf16), w_ple_ref[...], preferred_element_type=f32)
    gate = _sigmoid(jnp.dot(hb_scr[...], w_gate_ref[...], preferred_element_type=f32)
                    + b_gate_ref[...])
    out_ref[0] = h_scr[...] + gate * pe


def kernel(x, p, ln_emb_g, ln_emb_b, w_in, conv_w, conv_b, conv_ln_g, conv_ln_b, sgu_ln_g, sgu_ln_b, w_s, b_s, w_out, post_ln_g, post_ln_b, w_ple, w_ple_gate, b_ple_gate):
    batch, seq, d_model = x.shape
    assert d_model == D_MODEL and seq % TILE == 0 and p.shape[0] == DEPTH
    bf16 = jnp.bfloat16
    f32 = jnp.float32
    row = lambda v: v.reshape(1, -1).astype(f32)

    ws_pairs = w_s[0].reshape(N_PAIRS, 2, CHUNK, CHUNK).transpose(0, 2, 1, 3).reshape(
        N_PAIRS, CHUNK, 2 * CHUNK)
    bs_full = jnp.repeat(b_s[0].T, SGU_HEAD_DIM, axis=1)

    const = lambda shape: pl.BlockSpec(shape, lambda b, s: (0,) * len(shape))
    in_specs = [
        pl.BlockSpec((1, TILE, D_MODEL), lambda b, s: (b, s, 0)),
        pl.BlockSpec((1, TILE, D_PLE), lambda b, s: (b, s, 0)),
        const((1, D_MODEL)), const((1, D_MODEL)),
        const((D_MODEL, 3 * D_CONV + 3 * D_SGU)),
        const((CONV_WIDTH, D_CONV)), const((1, D_CONV)),
        const((1, D_CONV)), const((1, D_CONV)),
        const((1, D_SGU)), const((1, D_SGU)),
        const((N_PAIRS, CHUNK, 2 * CHUNK)), const((CHUNK, D_SGU)),
        const((D_MODEL, D_MODEL)),
        const((1, D_MODEL)), const((1, D_MODEL)),
        const((D_PLE, D_MODEL)), const((D_MODEL, D_MODEL)),
        const((1, D_MODEL)),
    ]
    scratch = [
        pltpu.VMEM((TILE, D_MODEL), f32),
        pltpu.VMEM((TILE, D_MODEL), bf16),
        pltpu.VMEM((N_CONV_SLABS, HALO + TILE, LANES), f32),
        pltpu.VMEM((TILE, D_CONV), f32),
        pltpu.VMEM((TILE, D_SGU), f32),
        pltpu.VMEM((TILE, D_SGU), bf16),
        pltpu.VMEM((TILE, D_SGU), f32),
        pltpu.VMEM((TILE, D_MODEL), bf16),
        pltpu.VMEM((N_PAIRS, CHUNK, 2 * CHUNK), bf16),
    ]
    return pl.pallas_call(
        _layer_kernel,
        grid=(batch, seq // TILE),
        in_specs=in_specs,
        out_specs=pl.BlockSpec((1, TILE, D_MODEL), lambda b, s: (b, s, 0)),
        out_shape=jax.ShapeDtypeStruct(x.shape, x.dtype),
        scratch_shapes=scratch,
        compiler_params=pltpu.CompilerParams(
            dimension_semantics=("arbitrary", "arbitrary"),
            vmem_limit_bytes=VMEM_LIMIT_BYTES),
        name="deepnorm_hybrid_layer",
    )(x, p[0], row(ln_emb_g), row(ln_emb_b), w_in[0].astype(bf16),
      conv_w[0].astype(f32), row(conv_b[0]), row(conv_ln_g[0]), row(conv_ln_b[0]),
      row(sgu_ln_g[0]), row(sgu_ln_b[0]), ws_pairs, bs_full, w_out[0].astype(bf16),
      row(post_ln_g[0]), row(post_ln_b[0]), w_ple[0].astype(bf16),
      w_ple_gate[0].astype(bf16), row(b_ple_gate[0]))
```

```python
import jax
import jax.numpy as jnp
from jax import lax
from jax.experimental import pallas as pl
from jax.experimental.pallas import tpu as pltpu

D_MODEL = 1024
D_PLE = 256
D_CONV = 512
D_SGU = 512
CONV_WIDTH = 31
N_SGU_HEADS = 8
SGU_HEAD_DIM = 64
CHUNK = 128
LN_EPS = 1e-5
DEPTH = 1
ALPHA = (2 * DEPTH) ** 0.25

LANES = 128
TILE = 512
ROWS = 64
HALO = 32
N_PAIRS = N_SGU_HEADS // 2
N_CONV_SLABS = D_CONV // LANES
VMEM_LIMIT_BYTES = 56 * 1024 * 1024

_INV_SQRT2 = 0.7071067811865476


def _ln(x, g, b):
    mu = jnp.mean(x, axis=-1, keepdims=True)
    xc = x - mu
    var = jnp.mean(xc * xc, axis=-1, keepdims=True)
    return xc * lax.rsqrt(var + LN_EPS) * g + b


def _silu(x):
    return x * jax.nn.sigmoid(x)


def _gelu(x):
    return 0.5 * x * (1.0 + lax.erf(x * _INV_SQRT2))


def _layer_kernel(x_ref, p_ref, ln_g_ref, ln_b_ref, w_in_ref, conv_w_ref, conv_b_ref,
                  cln_g_ref, cln_b_ref, sln_g_ref, sln_b_ref, ws_ref, bs_ref, w_out_ref,
                  pln_g_ref, pln_b_ref, w_ple_ref, w_gate_ref, b_gate_ref,
                  out_ref,
                  h_scr, a_scr, za_scr, u_scr, v_scr, zb_scr, y_scr, wsm_scr):
    s = pl.program_id(1)
    f32 = jnp.float32

    @pl.when((pl.program_id(0) == 0) & (s == 0))
    def _():
        row = lax.broadcasted_iota(jnp.int32, (CHUNK, 2 * CHUNK), 0)
        col = lax.broadcasted_iota(jnp.int32, (CHUNK, 2 * CHUNK), 1)
        keep = (col & (CHUNK - 1)) <= row
        for j in range(N_PAIRS):
            wsm_scr[j] = jnp.where(keep, ws_ref[j], 0.0)

    @pl.when(s == 0)
    def _():
        a_scr[:, 0:HALO, :] = jnp.zeros((N_CONV_SLABS, HALO, LANES), f32)

    for r in range(0, TILE, ROWS):
        h_scr[r:r + ROWS, :] = _ln(x_ref[0, r:r + ROWS, :], ln_g_ref[...], ln_b_ref[...])

    def proj(g):
        return jnp.dot(h_scr[...], w_in_ref[:, g * 512:(g + 1) * 512],
                       preferred_element_type=f32)

    a = proj(0) * jax.nn.sigmoid(proj(1))
    for c in range(N_CONV_SLABS):
        a_scr[c, HALO:HALO + TILE, :] = a[:, c * LANES:(c + 1) * LANES]
    za_scr[...] = _silu(proj(2))
    u_scr[...] = _gelu(proj(3))
    v_scr[...] = _ln(_gelu(proj(4)), sln_g_ref[...], sln_b_ref[...])
    zb_scr[...] = _silu(proj(5))

    for r in range(0, TILE, ROWS):
        slabs = []
        for c in range(N_CONV_SLABS):
            cols = slice(c * LANES, (c + 1) * LANES)
            acc = jnp.broadcast_to(conv_b_ref[:, cols], (ROWS, LANES))
            for k in range(CONV_WIDTH):
                off = r + HALO - (CONV_WIDTH - 1) + k
                acc = acc + conv_w_ref[k:k + 1, cols] * a_scr[c, off:off + ROWS, :]
            slabs.append(acc)
        conv = jnp.concatenate(slabs, axis=-1)
        ya = _silu(_ln(conv, cln_g_ref[...], cln_b_ref[...])) * za_scr[r:r + ROWS, :]
        y_scr[r:r + ROWS, 0:D_CONV] = ya
    a_scr[:, 0:HALO, :] = a_scr[:, TILE:TILE + HALO, :]

    lane = lax.broadcasted_iota(jnp.int32, (CHUNK, 2 * SGU_HEAD_DIM), 1)
    lo = lane < SGU_HEAD_DIM
    for c in range(0, TILE, CHUNK):
        for j in range(N_PAIRS):
            cols = slice(j * 2 * SGU_HEAD_DIM, (j + 1) * 2 * SGU_HEAD_DIM)
            vp = v_scr[c:c + CHUNK, cols]
            zero = jnp.zeros_like(vp)
            rhs = jnp.concatenate([jnp.where(lo, vp, zero), jnp.where(lo, zero, vp)], axis=0)
            sv = jnp.dot(wsm_scr[j], rhs, preferred_element_type=f32) + bs_ref[:, cols]
            yb = u_scr[c:c + CHUNK, cols] * sv * zb_scr[c:c + CHUNK, cols]
            y_scr[c:c + CHUNK, D_CONV + j * 128:D_CONV + (j + 1) * 128] = yb

    mix = jnp.dot(y_scr[...], w_out_ref[...], preferred_element_type=f32)
    h_scr[...] = _ln(ALPHA * h_scr[...] + mix, pln_g_ref[...], pln_b_ref[...])

    pe = jnp.dot(p_ref[0], w_ple_ref[...], preferred_element_type=f32)
    gate = jax.nn.sigmoid(jnp.dot(h_scr[...], w_gate_ref[...], preferred_element_type=f32)
                          + b_gate_ref[...])
    out_ref[0] = h_scr[...] + gate * pe


def kernel(x, p, ln_emb_g, ln_emb_b, w_in, conv_w, conv_b, conv_ln_g, conv_ln_b, sgu_ln_g, sgu_ln_b, w_s, b_s, w_out, post_ln_g, post_ln_b, w_ple, w_ple_gate, b_ple_gate):
    batch, seq, d_model = x.shape
    assert d_model == D_MODEL and seq % TILE == 0 and p.shape[0] == DEPTH
    f32 = jnp.float32
    row = lambda v: v.reshape(1, -1).astype(f32)

    ws_pairs = w_s[0].reshape(N_PAIRS, 2, CHUNK, CHUNK).transpose(0, 2, 1, 3).reshape(
        N_PAIRS, CHUNK, 2 * CHUNK)
    bs_full = jnp.repeat(b_s[0].T, SGU_HEAD_DIM, axis=1)

    const = lambda shape: pl.BlockSpec(shape, lambda b, s: (0,) * len(shape),
                                       pipeline_mode=pl.Buffered(1))
    in_specs = [
        pl.BlockSpec((1, TILE, D_MODEL), lambda b, s: (b, s, 0)),
        pl.BlockSpec((1, TILE, D_PLE), lambda b, s: (b, s, 0)),
        const((1, D_MODEL)), const((1, D_MODEL)),
        const((D_MODEL, 3 * D_CONV + 3 * D_SGU)),
        const((CONV_WIDTH, D_CONV)), const((1, D_CONV)),
        const((1, D_CONV)), const((1, D_CONV)),
        const((1, D_SGU)), const((1, D_SGU)),
        const((N_PAIRS, CHUNK, 2 * CHUNK)), const((CHUNK, D_SGU)),
        const((D_MODEL, D_MODEL)),
        const((1, D_MODEL)), const((1, D_MODEL)),
        const((D_PLE, D_MODEL)), const((D_MODEL, D_MODEL)),
        const((1, D_MODEL)),
    ]
    scratch = [
        pltpu.VMEM((TILE, D_MODEL), f32),
        pltpu.VMEM((N_CONV_SLABS, HALO + TILE, LANES), f32),
        pltpu.VMEM((TILE, D_CONV), f32),
        pltpu.VMEM((TILE, D_SGU), f32),
        pltpu.VMEM((TILE, D_SGU), f32),
        pltpu.VMEM((TILE, D_SGU), f32),
        pltpu.VMEM((TILE, D_MODEL), f32),
        pltpu.VMEM((N_PAIRS, CHUNK, 2 * CHUNK), f32),
    ]
    return pl.pallas_call(
        _layer_kernel,
        grid=(batch, seq // TILE),
        in_specs=in_specs,
        out_specs=pl.BlockSpec((1, TILE, D_MODEL), lambda b, s: (b, s, 0)),
        out_shape=jax.ShapeDtypeStruct(x.shape, x.dtype),
        scratch_shapes=scratch,
        compiler_params=pltpu.CompilerParams(
            dimension_semantics=("arbitrary", "arbitrary"),
            vmem_limit_bytes=VMEM_LIMIT_BYTES),
        name="deepnorm_hybrid_layer",
    )(x, p[0], row(ln_emb_g), row(ln_emb_b), w_in[0],
      conv_w[0], row(conv_b[0]), row(conv_ln_g[0]), row(conv_ln_b[0]),
      row(sgu_ln_g[0]), row(sgu_ln_b[0]), ws_pairs, bs_full, w_out[0],
      row(post_ln_g[0]), row(post_ln_b[0]), w_ple[0],
      w_ple_gate[0], row(b_ple_gate[0]))
```

```python
import jax
import jax.numpy as jnp
from jax import lax
from jax.experimental import pallas as pl
from jax.experimental.pallas import tpu as pltpu

D_MODEL = 1024
D_PLE = 256
D_CONV = 512
D_SGU = 512
CONV_WIDTH = 31
N_SGU_HEADS = 8
SGU_HEAD_DIM = 64
CHUNK = 128
LN_EPS = 1e-5
DEPTH = 1
ALPHA = (2 * DEPTH) ** 0.25

LANES = 128
TILE = 512
ROWS = 64
HALO = 32
N_PAIRS = N_SGU_HEADS // 2
N_SLABS = D_CONV // LANES
N_TAP_PAIRS = (CONV_WIDTH + 1) // 2
PACK = 16
SHIFTS = PACK // 2
VMEM_LIMIT_BYTES = 56 * 1024 * 1024

_INV_SQRT2 = 0.7071067811865476


def _ln(x, g, b):
    mu = jnp.mean(x, axis=-1, keepdims=True)
    xc = x - mu
    var = jnp.mean(xc * xc, axis=-1, keepdims=True)
    return xc * lax.rsqrt(var + LN_EPS) * g + b


def _silu(x):
    return x * jax.nn.sigmoid(x)


def _gelu(x):
    return 0.5 * x * (1.0 + lax.erf(x * _INV_SQRT2))


def _layer_kernel(x_ref, p_ref, ln_g_ref, ln_b_ref, w_in_ref, conv_w_ref, conv_b_ref,
                  cln_g_ref, cln_b_ref, sln_g_ref, sln_b_ref, ws_ref, bs_ref, w_out_ref,
                  pln_g_ref, pln_b_ref, w_ple_ref, w_gate_ref, b_gate_ref,
                  out_ref,
                  h_scr, a_scr, ash_scr, za_scr, ya_scr, u_scr, v_scr, zb_scr,
                  y_scr, wsm_scr, wp_scr):
    s = pl.program_id(1)
    bf16 = jnp.bfloat16
    f32 = jnp.float32

    @pl.when((pl.program_id(0) == 0) & (s == 0))
    def _():
        row = lax.broadcasted_iota(jnp.int32, (CHUNK, 2 * CHUNK), 0)
        col = lax.broadcasted_iota(jnp.int32, (CHUNK, 2 * CHUNK), 1)
        keep = (col & (CHUNK - 1)) <= row
        for j in range(N_PAIRS):
            wsm_scr[j] = jnp.where(keep, ws_ref[j], 0.0)
        even_row = (lax.broadcasted_iota(jnp.int32, (PACK, LANES), 0) & 1) == 0
        zero_row = jnp.zeros((1, LANES), f32)

        def tap(k, cols):
            return conv_w_ref[k:k + 1, cols] if 0 <= k < CONV_WIDTH else zero_row

        for c in range(N_SLABS):
            cols = slice(c * LANES, (c + 1) * LANES)
            for m in range(N_TAP_PAIRS):
                for par in range(2):
                    k_lo = 2 * m - par
                    pair = jnp.where(even_row,
                                     jnp.broadcast_to(tap(k_lo, cols), (PACK, LANES)),
                                     jnp.broadcast_to(tap(k_lo + 1, cols), (PACK, LANES)))
                    wp_scr[(c * N_TAP_PAIRS + m) * 2 + par] = pair.astype(bf16)
        a_scr[:, HALO + TILE:, :] = jnp.zeros((N_SLABS, PACK, LANES), f32)

    @pl.when(s == 0)
    def _():
        a_scr[:, 0:HALO, :] = jnp.zeros((N_SLABS, HALO, LANES), f32)

    for r in range(0, TILE, ROWS):
        h_scr[r:r + ROWS, :] = _ln(x_ref[0, r:r + ROWS, :], ln_g_ref[...], ln_b_ref[...])

    def proj(g):
        return jnp.dot(h_scr[...], w_in_ref[:, g * 512:(g + 1) * 512],
                       preferred_element_type=f32)

    a = proj(0) * jax.nn.sigmoid(proj(1))
    for c in range(N_SLABS):
        a_scr[c, HALO:HALO + TILE, :] = a[:, c * LANES:(c + 1) * LANES]
    for c in range(N_SLABS):
        for j in range(SHIFTS):
            ash_scr[j, c] = a_scr[c, 2 * j:2 * j + HALO + TILE, :].astype(bf16)
    za = _silu(proj(2))
    for c in range(N_SLABS):
        za_scr[c] = za[:, c * LANES:(c + 1) * LANES]
    u_scr[...] = _gelu(proj(3))
    v_scr[...] = _ln(_gelu(proj(4)), sln_g_ref[...], sln_b_ref[...])
    zb_scr[...] = _silu(proj(5))

    half = ROWS // 2
    groups = ROWS // PACK
    pair_sum = (lax.broadcasted_iota(jnp.int32, (half, ROWS), 1) // 2
                == lax.broadcasted_iota(jnp.int32, (half, ROWS), 0)).astype(bf16)
    for r in range(0, TILE, ROWS):
        conv = [[None] * N_SLABS for _ in range(2)]
        for c in range(N_SLABS):
            acc = [[None] * groups for _ in range(2)]
            for m in range(N_TAP_PAIRS):
                off = r + HALO - 2 * (N_TAP_PAIRS - 1) + 2 * m
                shift, base = (off % PACK) // 2, off - off % PACK
                win = ash_scr[shift, c, base:base + ROWS, :]
                for par in range(2):
                    wm = wp_scr[(c * N_TAP_PAIRS + m) * 2 + par]
                    for g in range(groups):
                        prod = win[PACK * g:PACK * (g + 1), :] * wm
                        acc[par][g] = prod if acc[par][g] is None else acc[par][g] + prod
            bias = conv_b_ref[:, c * LANES:(c + 1) * LANES]
            for par in range(2):
                conv[par][c] = jnp.dot(pair_sum, jnp.concatenate(acc[par], axis=0),
                                       preferred_element_type=f32) + bias
        for par in range(2):
            rows = pl.ds(r + par, half, stride=2)
            z = jnp.concatenate([za_scr[c, rows, :] for c in range(N_SLABS)], axis=-1)
            ya = _silu(_ln(jnp.concatenate(conv[par], axis=-1), cln_g_ref[...], cln_b_ref[...])) * z
            for c in range(N_SLABS):
                ya_scr[c, rows, :] = ya[:, c * LANES:(c + 1) * LANES]
    a_scr[:, 0:HALO, :] = a_scr[:, TILE:TILE + HALO, :]
    for c in range(N_SLABS):
        y_scr[:, c * LANES:(c + 1) * LANES] = ya_scr[c]

    lane = lax.broadcasted_iota(jnp.int32, (CHUNK, 2 * SGU_HEAD_DIM), 1)
    lo_lanes = lane < SGU_HEAD_DIM
    for c in range(0, TILE, CHUNK):
        for j in range(N_PAIRS):
            cols = slice(j * 2 * SGU_HEAD_DIM, (j + 1) * 2 * SGU_HEAD_DIM)
            vp = v_scr[c:c + CHUNK, cols]
            zero = jnp.zeros_like(vp)
            rhs = jnp.concatenate([jnp.where(lo_lanes, vp, zero), jnp.where(lo_lanes, zero, vp)],
                                  axis=0)
            sv = jnp.dot(wsm_scr[j], rhs, preferred_element_type=f32) + bs_ref[:, cols]
            yb = u_scr[c:c + CHUNK, cols] * sv * zb_scr[c:c + CHUNK, cols]
            y_scr[c:c + CHUNK, D_CONV + j * 128:D_CONV + (j + 1) * 128] = yb

    mix = jnp.dot(y_scr[...], w_out_ref[...], preferred_element_type=f32)
    h_scr[...] = _ln(ALPHA * h_scr[...] + mix, pln_g_ref[...], pln_b_ref[...])

    pe = jnp.dot(p_ref[0], w_ple_ref[...], preferred_element_type=f32)
    gate = jax.nn.sigmoid(jnp.dot(h_scr[...], w_gate_ref[...], preferred_element_type=f32)
                          + b_gate_ref[...])
    out_ref[0] = h_scr[...] + gate * pe


def kernel(x, p, ln_emb_g, ln_emb_b, w_in, conv_w, conv_b, conv_ln_g, conv_ln_b, sgu_ln_g, sgu_ln_b, w_s, b_s, w_out, post_ln_g, post_ln_b, w_ple, w_ple_gate, b_ple_gate):
    batch, seq, d_model = x.shape
    assert d_model == D_MODEL and seq % TILE == 0 and p.shape[0] == DEPTH
    bf16 = jnp.bfloat16
    f32 = jnp.float32
    row = lambda v: v.reshape(1, -1).astype(f32)

    ws_pairs = w_s[0].reshape(N_PAIRS, 2, CHUNK, CHUNK).transpose(0, 2, 1, 3).reshape(
        N_PAIRS, CHUNK, 2 * CHUNK)
    bs_full = jnp.repeat(b_s[0].T, SGU_HEAD_DIM, axis=1)

    const = lambda shape: pl.BlockSpec(shape, lambda b, s: (0,) * len(shape),
                                       pipeline_mode=pl.Buffered(1))
    in_specs = [
        pl.BlockSpec((1, TILE, D_MODEL), lambda b, s: (b, s, 0)),
        pl.BlockSpec((1, TILE, D_PLE), lambda b, s: (b, s, 0)),
        const((1, D_MODEL)), const((1, D_MODEL)),
        const((D_MODEL, 3 * D_CONV + 3 * D_SGU)),
        const((CONV_WIDTH, D_CONV)), const((1, D_CONV)),
        const((1, D_CONV)), const((1, D_CONV)),
        const((1, D_SGU)), const((1, D_SGU)),
        const((N_PAIRS, CHUNK, 2 * CHUNK)), const((CHUNK, D_SGU)),
        const((D_MODEL, D_MODEL)),
        const((1, D_MODEL)), const((1, D_MODEL)),
        const((D_PLE, D_MODEL)), const((D_MODEL, D_MODEL)),
        const((1, D_MODEL)),
    ]
    scratch = [
        pltpu.VMEM((TILE, D_MODEL), f32),
        pltpu.VMEM((N_SLABS, HALO + TILE + PACK, LANES), f32),
        pltpu.VMEM((SHIFTS, N_SLABS, HALO + TILE, LANES), bf16),
        pltpu.VMEM((N_SLABS, TILE, LANES), f32),
        pltpu.VMEM((N_SLABS, TILE, LANES), f32),
        pltpu.VMEM((TILE, D_SGU), f32),
        pltpu.VMEM((TILE, D_SGU), f32),
        pltpu.VMEM((TILE, D_SGU), f32),
        pltpu.VMEM((TILE, D_MODEL), f32),
        pltpu.VMEM((N_PAIRS, CHUNK, 2 * CHUNK), f32),
        pltpu.VMEM((N_SLABS * N_TAP_PAIRS * 2, PACK, LANES), bf16),
    ]
    return pl.pallas_call(
        _layer_kernel,
        grid=(batch, seq // TILE),
        in_specs=in_specs,
        out_specs=pl.BlockSpec((1, TILE, D_MODEL), lambda b, s: (b, s, 0)),
        out_shape=jax.ShapeDtypeStruct(x.shape, x.dtype),
        scratch_shapes=scratch,
        compiler_params=pltpu.CompilerParams(
            dimension_semantics=("arbitrary", "arbitrary"),
            vmem_limit_bytes=VMEM_LIMIT_BYTES),
        name="deepnorm_hybrid_layer",
    )(x, p[0], row(ln_emb_g), row(ln_emb_b), w_in[0],
      conv_w[0], row(conv_b[0]), row(conv_ln_g[0]), row(conv_ln_b[0]),
      row(sgu_ln_g[0]), row(sgu_ln_b[0]), ws_pairs, bs_full, w_out[0],
      row(post_ln_g[0]), row(post_ln_b[0]), w_ple[0],
      w_ple_gate[0], row(b_ple_gate[0]))
```

```python
import jax
import jax.numpy as jnp
from jax import lax
from jax.experimental import pallas as pl
from jax.experimental.pallas import tpu as pltpu

D_MODEL = 1024
D_PLE = 256
D_CONV = 512
D_SGU = 512
CONV_WIDTH = 31
N_SGU_HEADS = 8
SGU_HEAD_DIM = 64
CHUNK = 128
LN_EPS = 1e-5
DEPTH = 1
ALPHA = (2 * DEPTH) ** 0.25

LANES = 128
TILE = 512
ROWS = 64
HALO = 32
N_PAIRS = N_SGU_HEADS // 2
N_SLABS = D_CONV // LANES
PACK = 16
VMEM_LIMIT_BYTES = 56 * 1024 * 1024

_INV_SQRT2 = 0.7071067811865476


def _ln(x, g, b):
    mu = jnp.mean(x, axis=-1, keepdims=True)
    xc = x - mu
    var = jnp.mean(xc * xc, axis=-1, keepdims=True)
    return xc * lax.rsqrt(var + LN_EPS) * g + b


def _silu(x):
    return x * jax.nn.sigmoid(x)


def _gelu(x):
    return 0.5 * x * (1.0 + lax.erf(x * _INV_SQRT2))


def _layer_kernel(x_ref, p_ref, ln_g_ref, ln_b_ref, w_in_ref, conv_w_ref, conv_b_ref,
                  cln_g_ref, cln_b_ref, sln_g_ref, sln_b_ref, ws_ref, bs_ref, w_out_ref,
                  pln_g_ref, pln_b_ref, w_ple_ref, w_gate_ref, b_gate_ref,
                  out_ref,
                  h_scr, a_scr, ash_scr, za_scr, u_scr, v_scr, zb_scr, y_scr, wsm_scr, wb_scr):
    s = pl.program_id(1)
    bf16 = jnp.bfloat16
    f32 = jnp.float32

    @pl.when((pl.program_id(0) == 0) & (s == 0))
    def _():
        row = lax.broadcasted_iota(jnp.int32, (CHUNK, 2 * CHUNK), 0)
        col = lax.broadcasted_iota(jnp.int32, (CHUNK, 2 * CHUNK), 1)
        keep = (col & (CHUNK - 1)) <= row
        for j in range(N_PAIRS):
            wsm_scr[j] = jnp.where(keep, ws_ref[j], 0.0)
        for k in range(CONV_WIDTH):
            wb_scr[k] = jnp.broadcast_to(conv_w_ref[k:k + 1, :], (PACK, D_CONV)).astype(bf16)
        a_scr[:, HALO + TILE:, :] = jnp.zeros((N_SLABS, PACK, LANES), f32)

    @pl.when(s == 0)
    def _():
        a_scr[:, 0:HALO, :] = jnp.zeros((N_SLABS, HALO, LANES), f32)

    for r in range(0, TILE, ROWS):
        h_scr[r:r + ROWS, :] = _ln(x_ref[0, r:r + ROWS, :], ln_g_ref[...], ln_b_ref[...])

    def proj(g):
        return jnp.dot(h_scr[...], w_in_ref[:, g * 512:(g + 1) * 512],
                       preferred_element_type=f32)

    a = proj(0) * jax.nn.sigmoid(proj(1))
    for c in range(N_SLABS):
        a_scr[c, HALO:HALO + TILE, :] = a[:, c * LANES:(c + 1) * LANES]
    for c in range(N_SLABS):
        for j in range(PACK):
            ash_scr[j, c] = a_scr[c, j:j + HALO + TILE, :].astype(bf16)
    za_scr[...] = _silu(proj(2))
    u_scr[...] = _gelu(proj(3))
    v_scr[...] = _ln(_gelu(proj(4)), sln_g_ref[...], sln_b_ref[...])
    zb_scr[...] = _silu(proj(5))

    groups = ROWS // PACK
    for r in range(0, TILE, ROWS):
        slabs = []
        for c in range(N_SLABS):
            cols = slice(c * LANES, (c + 1) * LANES)
            acc = [None] * groups
            for k in range(CONV_WIDTH):
                off = r + HALO - (CONV_WIDTH - 1) + k
                shift, base = off % PACK, off - off % PACK
                win = ash_scr[shift, c, base:base + ROWS, :]
                wk = wb_scr[k, :, cols]
                for g in range(groups):
                    prod = win[PACK * g:PACK * (g + 1), :] * wk
                    acc[g] = prod if acc[g] is None else acc[g] + prod
            slabs.append(jnp.concatenate(acc, axis=0).astype(f32) + conv_b_ref[:, cols])
        conv = jnp.concatenate(slabs, axis=-1)
        ya = _silu(_ln(conv, cln_g_ref[...], cln_b_ref[...])) * za_scr[r:r + ROWS, :]
        y_scr[r:r + ROWS, 0:D_CONV] = ya
    a_scr[:, 0:HALO, :] = a_scr[:, TILE:TILE + HALO, :]

    lane = lax.broadcasted_iota(jnp.int32, (CHUNK, 2 * SGU_HEAD_DIM), 1)
    lo_lanes = lane < SGU_HEAD_DIM
    for c in range(0, TILE, CHUNK):
        for j in range(N_PAIRS):
            cols = slice(j * 2 * SGU_HEAD_DIM, (j + 1) * 2 * SGU_HEAD_DIM)
            vp = v_scr[c:c + CHUNK, cols]
            zero = jnp.zeros_like(vp)
            rhs = jnp.concatenate([jnp.where(lo_lanes, vp, zero), jnp.where(lo_lanes, zero, vp)],
                                  axis=0)
            sv = jnp.dot(wsm_scr[j], rhs, preferred_element_type=f32) + bs_ref[:, cols]
            yb = u_scr[c:c + CHUNK, cols] * sv * zb_scr[c:c + CHUNK, cols]
            y_scr[c:c + CHUNK, D_CONV + j * 128:D_CONV + (j + 1) * 128] = yb

    mix = jnp.dot(y_scr[...], w_out_ref[...], preferred_element_type=f32)
    h_scr[...] = _ln(ALPHA * h_scr[...] + mix, pln_g_ref[...], pln_b_ref[...])

    pe = jnp.dot(p_ref[0], w_ple_ref[...], preferred_element_type=f32)
    gate = jax.nn.sigmoid(jnp.dot(h_scr[...], w_gate_ref[...], preferred_element_type=f32)
                          + b_gate_ref[...])
    out_ref[0] = h_scr[...] + gate * pe


def kernel(x, p, ln_emb_g, ln_emb_b, w_in, conv_w, conv_b, conv_ln_g, conv_ln_b, sgu_ln_g, sgu_ln_b, w_s, b_s, w_out, post_ln_g, post_ln_b, w_ple, w_ple_gate, b_ple_gate):
    batch, seq, d_model = x.shape
    assert d_model == D_MODEL and seq % TILE == 0 and p.shape[0] == DEPTH
    bf16 = jnp.bfloat16
    f32 = jnp.float32
    row = lambda v: v.reshape(1, -1).astype(f32)

    ws_pairs = w_s[0].reshape(N_PAIRS, 2, CHUNK, CHUNK).transpose(0, 2, 1, 3).reshape(
        N_PAIRS, CHUNK, 2 * CHUNK)
    bs_full = jnp.repeat(b_s[0].T, SGU_HEAD_DIM, axis=1)

    const = lambda shape: pl.BlockSpec(shape, lambda b, s: (0,) * len(shape),
                                       pipeline_mode=pl.Buffered(1))
    in_specs = [
        pl.BlockSpec((1, TILE, D_MODEL), lambda b, s: (b, s, 0)),
        pl.BlockSpec((1, TILE, D_PLE), lambda b, s: (b, s, 0)),
        const((1, D_MODEL)), const((1, D_MODEL)),
        const((D_MODEL, 3 * D_CONV + 3 * D_SGU)),
        const((CONV_WIDTH, D_CONV)), const((1, D_CONV)),
        const((1, D_CONV)), const((1, D_CONV)),
        const((1, D_SGU)), const((1, D_SGU)),
        const((N_PAIRS, CHUNK, 2 * CHUNK)), const((CHUNK, D_SGU)),
        const((D_MODEL, D_MODEL)),
        const((1, D_MODEL)), const((1, D_MODEL)),
        const((D_PLE, D_MODEL)), const((D_MODEL, D_MODEL)),
        const((1, D_MODEL)),
    ]
    scratch = [
        pltpu.VMEM((TILE, D_MODEL), f32),
        pltpu.VMEM((N_SLABS, HALO + TILE + PACK, LANES), f32),
        pltpu.VMEM((PACK, N_SLABS, HALO + TILE, LANES), bf16),
        pltpu.VMEM((TILE, D_CONV), f32),
        pltpu.VMEM((TILE, D_SGU), f32),
        pltpu.VMEM((TILE, D_SGU), f32),
        pltpu.VMEM((TILE, D_SGU), f32),
        pltpu.VMEM((TILE, D_MODEL), f32),
        pltpu.VMEM((N_PAIRS, CHUNK, 2 * CHUNK), f32),
        pltpu.VMEM((CONV_WIDTH, PACK, D_CONV), bf16),
    ]
    return pl.pallas_call(
        _layer_kernel,
        grid=(batch, seq // TILE),
        in_specs=in_specs,
        out_specs=pl.BlockSpec((1, TILE, D_MODEL), lambda b, s: (b, s, 0)),
        out_shape=jax.ShapeDtypeStruct(x.shape, x.dtype),
        scratch_shapes=scratch,
        compiler_params=pltpu.CompilerParams(
            dimension_semantics=("arbitrary", "arbitrary"),
            vmem_limit_bytes=VMEM_LIMIT_BYTES),
        name="deepnorm_hybrid_layer",
    )(x, p[0], row(ln_emb_g), row(ln_emb_b), w_in[0],
      conv_w[0], row(conv_b[0]), row(conv_ln_g[0]), row(conv_ln_b[0]),
      row(sgu_ln_g[0]), row(sgu_ln_b[0]), ws_pairs, bs_full, w_out[0],
      row(post_ln_g[0]), row(post_ln_b[0]), w_ple[0],
      w_ple_gate[0], row(b_ple_gate[0]))
```

```python
import jax
import jax.numpy as jnp
from jax import lax
from jax.experimental import pallas as pl
from jax.experimental.pallas import tpu as pltpu

D_MODEL = 1024
D_PLE = 256
D_CONV = 512
D_SGU = 512
CONV_WIDTH = 31
N_SGU_HEADS = 8
SGU_HEAD_DIM = 64
CHUNK = 128
LN_EPS = 1e-5
DEPTH = 1
ALPHA = (2 * DEPTH) ** 0.25

LANES = 128
TILE = 512
ROWS = 64
HALO = 32
N_PHASE_TAPS = (CONV_WIDTH + 1) // 2
P2_PAD = 8
N_PAIRS = N_SGU_HEADS // 2
N_CONV_SLABS = D_CONV // LANES
VMEM_LIMIT_BYTES = 56 * 1024 * 1024

_INV_SQRT2 = 0.7071067811865476


def _ln(x, g, b):
    mu = jnp.mean(x, axis=-1, keepdims=True)
    xc = x - mu
    var = jnp.mean(xc * xc, axis=-1, keepdims=True)
    return xc * lax.rsqrt(var + LN_EPS) * g + b


def _silu(x):
    return x * jax.nn.sigmoid(x)


def _gelu(x):
    return 0.5 * x * (1.0 + lax.erf(x * _INV_SQRT2))


def _layer_kernel(x_ref, p_ref, ln_g_ref, ln_b_ref, w_in_ref, conv_w_ref, conv_b_ref,
                  cln_g_ref, cln_b_ref, sln_g_ref, sln_b_ref, ws_ref, bs_ref, w_out_ref,
                  pln_g_ref, pln_b_ref, w_ple_ref, w_gate_ref, b_gate_ref,
                  out_ref,
                  h_scr, a_scr, xe_scr, xo_scr, xs_scr, p2_scr, za_scr, ya_scr, u_scr, v_scr,
                  zb_scr, y_scr, wsm_scr, hw_scr):
    s = pl.program_id(1)
    f32 = jnp.float32

    @pl.when((pl.program_id(0) == 0) & (s == 0))
    def _():
        row = lax.broadcasted_iota(jnp.int32, (CHUNK, 2 * CHUNK), 0)
        col = lax.broadcasted_iota(jnp.int32, (CHUNK, 2 * CHUNK), 1)
        keep = (col & (CHUNK - 1)) <= row
        for j in range(N_PAIRS):
            wsm_scr[j] = jnp.where(keep, ws_ref[j], 0.0)
        for m in range(N_PHASE_TAPS):
            k0, k1 = CONV_WIDTH - 1 - 2 * m, CONV_WIDTH - 2 - 2 * m
            h0 = conv_w_ref[k0:k0 + 1, :]
            h1 = conv_w_ref[k1:k1 + 1, :] if k1 >= 0 else jnp.zeros((1, D_CONV), f32)
            hw_scr[0, m:m + 1, :] = h0
            hw_scr[1, m:m + 1, :] = h1
            hw_scr[2, m:m + 1, :] = h0 + h1

    @pl.when(s == 0)
    def _():
        a_scr[:, 0:HALO, :] = jnp.zeros((N_CONV_SLABS, HALO, LANES), f32)
        p2_scr[:, 0:P2_PAD, :] = jnp.zeros((N_CONV_SLABS, P2_PAD, LANES), f32)

    for r in range(0, TILE, ROWS):
        h_scr[r:r + ROWS, :] = _ln(x_ref[0, r:r + ROWS, :], ln_g_ref[...], ln_b_ref[...])

    def proj(g):
        return jnp.dot(h_scr[...], w_in_ref[:, g * 512:(g + 1) * 512],
                       preferred_element_type=f32)

    a = proj(0) * jax.nn.sigmoid(proj(1))
    for c in range(N_CONV_SLABS):
        a_scr[c, HALO:HALO + TILE, :] = a[:, c * LANES:(c + 1) * LANES]
    n_half = (HALO + TILE) // 2
    for c in range(N_CONV_SLABS):
        xe = a_scr[c, pl.ds(0, n_half, stride=2), :]
        xo = a_scr[c, pl.ds(1, n_half, stride=2), :]
        xe_scr[c] = xe
        xo_scr[c] = xo
        xs_scr[c] = xe + xo
    za = _silu(proj(2))
    for c in range(N_CONV_SLABS):
        za_scr[c] = za[:, c * LANES:(c + 1) * LANES]
    u_scr[...] = _gelu(proj(3))
    v_scr[...] = _ln(_gelu(proj(4)), sln_g_ref[...], sln_b_ref[...])
    zb_scr[...] = _silu(proj(5))

    half = ROWS // 2

    def phase_conv(x_scr, c, phase, q0, n_taps):
        cols = slice(c * LANES, (c + 1) * LANES)
        acc = None
        for m in range(n_taps):
            term = hw_scr[phase, m:m + 1, cols] * x_scr[c, q0 - m:q0 - m + half, :]
            acc = term if acc is None else acc + term
        return acc

    for r in range(0, TILE, ROWS):
        y_even, y_odd = [], []
        for c in range(N_CONV_SLABS):
            cols = slice(c * LANES, (c + 1) * LANES)
            q0 = (HALO + r) // 2
            p1 = phase_conv(xe_scr, c, 0, q0, N_PHASE_TAPS)
            p2 = phase_conv(xo_scr, c, 1, q0, N_PHASE_TAPS - 1)
            p3 = phase_conv(xs_scr, c, 2, q0, N_PHASE_TAPS)
            p2_scr[c, P2_PAD + r // 2:P2_PAD + r // 2 + half, :] = p2
            p2_prev = p2_scr[c, P2_PAD - 1 + r // 2:P2_PAD - 1 + r // 2 + half, :]
            bias = conv_b_ref[:, cols]
            y_even.append(p1 + p2_prev + bias)
            y_odd.append(p3 - p1 - p2 + bias)
        for par, parts in enumerate((y_even, y_odd)):
            rows = pl.ds(r + par, half, stride=2)
            z = jnp.concatenate([za_scr[c, rows, :] for c in range(N_CONV_SLABS)], axis=-1)
            ya = _silu(_ln(jnp.concatenate(parts, axis=-1), cln_g_ref[...], cln_b_ref[...])) * z
            for c in range(N_CONV_SLABS):
                ya_scr[c, rows, :] = ya[:, c * LANES:(c + 1) * LANES]
    a_scr[:, 0:HALO, :] = a_scr[:, TILE:TILE + HALO, :]
    p2_scr[:, P2_PAD - 1:P2_PAD, :] = p2_scr[:, P2_PAD + TILE // 2 - 1:P2_PAD + TILE // 2, :]
    for c in range(N_CONV_SLABS):
        y_scr[:, c * LANES:(c + 1) * LANES] = ya_scr[c]

    lane = lax.broadcasted_iota(jnp.int32, (CHUNK, 2 * SGU_HEAD_DIM), 1)
    lo = lane < SGU_HEAD_DIM
    for c in range(0, TILE, CHUNK):
        for j in range(N_PAIRS):
            cols = slice(j * 2 * SGU_HEAD_DIM, (j + 1) * 2 * SGU_HEAD_DIM)
            vp = v_scr[c:c + CHUNK, cols]
            zero = jnp.zeros_like(vp)
            rhs = jnp.concatenate([jnp.where(lo, vp, zero), jnp.where(lo, zero, vp)], axis=0)
            sv = jnp.dot(wsm_scr[j], rhs, preferred_element_type=f32) + bs_ref[:, cols]
            yb = u_scr[c:c + CHUNK, cols] * sv * zb_scr[c:c + CHUNK, cols]
            y_scr[c:c + CHUNK, D_CONV + j * 128:D_CONV + (j + 1) * 128] = yb

    mix = jnp.dot(y_scr[...], w_out_ref[...], preferred_element_type=f32)
    h_scr[...] = _ln(ALPHA * h_scr[...] + mix, pln_g_ref[...], pln_b_ref[...])

    pe = jnp.dot(p_ref[0], w_ple_ref[...], preferred_element_type=f32)
    gate = jax.nn.sigmoid(jnp.dot(h_scr[...], w_gate_ref[...], preferred_element_type=f32)
                          + b_gate_ref[...])
    out_ref[0] = h_scr[...] + gate * pe


def kernel(x, p, ln_emb_g, ln_emb_b, w_in, conv_w, conv_b, conv_ln_g, conv_ln_b, sgu_ln_g, sgu_ln_b, w_s, b_s, w_out, post_ln_g, post_ln_b, w_ple, w_ple_gate, b_ple_gate):
    batch, seq, d_model = x.shape
    assert d_model == D_MODEL and seq % TILE == 0 and p.shape[0] == DEPTH
    f32 = jnp.float32
    row = lambda v: v.reshape(1, -1).astype(f32)

    ws_pairs = w_s[0].reshape(N_PAIRS, 2, CHUNK, CHUNK).transpose(0, 2, 1, 3).reshape(
        N_PAIRS, CHUNK, 2 * CHUNK)
    bs_full = jnp.repeat(b_s[0].T, SGU_HEAD_DIM, axis=1)

    const = lambda shape: pl.BlockSpec(shape, lambda b, s: (0,) * len(shape),
                                       pipeline_mode=pl.Buffered(1))
    in_specs = [
        pl.BlockSpec((1, TILE, D_MODEL), lambda b, s: (b, s, 0)),
        pl.BlockSpec((1, TILE, D_PLE), lambda b, s: (b, s, 0)),
        const((1, D_MODEL)), const((1, D_MODEL)),
        const((D_MODEL, 3 * D_CONV + 3 * D_SGU)),
        const((CONV_WIDTH, D_CONV)), const((1, D_CONV)),
        const((1, D_CONV)), const((1, D_CONV)),
        const((1, D_SGU)), const((1, D_SGU)),
        const((N_PAIRS, CHUNK, 2 * CHUNK)), const((CHUNK, D_SGU)),
        const((D_MODEL, D_MODEL)),
        const((1, D_MODEL)), const((1, D_MODEL)),
        const((D_PLE, D_MODEL)), const((D_MODEL, D_MODEL)),
        const((1, D_MODEL)),
    ]
    scratch = [
        pltpu.VMEM((TILE, D_MODEL), f32),
        pltpu.VMEM((N_CONV_SLABS, HALO + TILE, LANES), f32),
        pltpu.VMEM((N_CONV_SLABS, (HALO + TILE) // 2, LANES), f32),
        pltpu.VMEM((N_CONV_SLABS, (HALO + TILE) // 2, LANES), f32),
        pltpu.VMEM((N_CONV_SLABS, (HALO + TILE) // 2, LANES), f32),
        pltpu.VMEM((N_CONV_SLABS, P2_PAD + TILE // 2, LANES), f32),
        pltpu.VMEM((N_CONV_SLABS, TILE, LANES), f32),
        pltpu.VMEM((N_CONV_SLABS, TILE, LANES), f32),
        pltpu.VMEM((TILE, D_SGU), f32),
        pltpu.VMEM((TILE, D_SGU), f32),
        pltpu.VMEM((TILE, D_SGU), f32),
        pltpu.VMEM((TILE, D_MODEL), f32),
        pltpu.VMEM((N_PAIRS, CHUNK, 2 * CHUNK), f32),
        pltpu.VMEM((3, N_PHASE_TAPS, D_CONV), f32),
    ]
    return pl.pallas_call(
        _layer_kernel,
        grid=(batch, seq // TILE),
        in_specs=in_specs,
        out_specs=pl.BlockSpec((1, TILE, D_MODEL), lambda b, s: (b, s, 0)),
        out_shape=jax.ShapeDtypeStruct(x.shape, x.dtype),
        scratch_shapes=scratch,
        compiler_params=pltpu.CompilerParams(
            dimension_semantics=("arbitrary", "arbitrary"),
            vmem_limit_bytes=VMEM_LIMIT_BYTES),
        name="deepnorm_hybrid_layer",
    )(x, p[0], row(ln_emb_g), row(ln_emb_b), w_in[0],
      conv_w[0], row(conv_b[0]), row(conv_ln_g[0]), row(conv_ln_b[0]),
      row(sgu_ln_g[0]), row(sgu_ln_b[0]), ws_pairs, bs_full, w_out[0],
      row(post_ln_g[0]), row(post_ln_b[0]), w_ple[0],
      w_ple_gate[0], row(b_ple_gate[0]))
```

```python
import jax
import jax.numpy as jnp
from jax import lax
from jax.experimental import pallas as pl
from jax.experimental.pallas import tpu as pltpu

D_MODEL = 1024
D_PLE = 256
D_CONV = 512
D_SGU = 512
CONV_WIDTH = 31
N_SGU_HEADS = 8
SGU_HEAD_DIM = 64
CHUNK = 128
LN_EPS = 1e-5
DEPTH = 1
ALPHA = (2 * DEPTH) ** 0.25

LANES = 128
TILE = 512
ROWS = 64
HALO = 32
N_PAIRS = N_SGU_HEADS // 2
N_CONV_SLABS = D_CONV // LANES
VMEM_LIMIT_BYTES = 56 * 1024 * 1024

_INV_SQRT2 = 0.7071067811865476


def _ln(x, g, b):
    mu = jnp.mean(x, axis=-1, keepdims=True)
    xc = x - mu
    var = jnp.mean(xc * xc, axis=-1, keepdims=True)
    return xc * lax.rsqrt(var + LN_EPS) * g + b


def _silu(x):
    return x * jax.nn.sigmoid(x)


def _gelu(x):
    return 0.5 * x * (1.0 + lax.erf(x * _INV_SQRT2))


def _layer_kernel(x_ref, p_ref, ln_g_ref, ln_b_ref, w_in_ref, conv_w_ref, conv_b_ref,
                  cln_g_ref, cln_b_ref, sln_g_ref, sln_b_ref, ws_ref, bs_ref, w_out_ref,
                  pln_g_ref, pln_b_ref, w_ple_ref, w_gate_ref, b_gate_ref,
                  out_ref,
                  h_scr, a_scr, za_scr, u_scr, v_scr, zb_scr, y_scr, wsm_scr, bsf_scr):
    s = pl.program_id(1)
    f32 = jnp.float32

    @pl.when((pl.program_id(0) == 0) & (s == 0))
    def _():
        row = lax.broadcasted_iota(jnp.int32, (CHUNK, CHUNK), 0)
        col = lax.broadcasted_iota(jnp.int32, (CHUNK, CHUNK), 1)
        keep = col <= row
        for j in range(N_PAIRS):
            wsm_scr[j] = jnp.concatenate(
                [jnp.where(keep, ws_ref[0, 2 * j + n], 0.0) for n in range(2)], axis=1)
        bias_t = bs_ref[0].T
        for hd in range(N_SGU_HEADS):
            bsf_scr[:, hd * SGU_HEAD_DIM:(hd + 1) * SGU_HEAD_DIM] = jnp.broadcast_to(
                bias_t[:, hd:hd + 1], (CHUNK, SGU_HEAD_DIM))

    @pl.when(s == 0)
    def _():
        a_scr[:, 0:HALO, :] = jnp.zeros((N_CONV_SLABS, HALO, LANES), f32)

    for r in range(0, TILE, ROWS):
        h_scr[r:r + ROWS, :] = _ln(x_ref[0, r:r + ROWS, :], ln_g_ref[...], ln_b_ref[...])

    def proj(g):
        return jnp.dot(h_scr[...], w_in_ref[0, :, g * 512:(g + 1) * 512],
                       preferred_element_type=f32)

    a = proj(0) * jax.nn.sigmoid(proj(1))
    for c in range(N_CONV_SLABS):
        a_scr[c, HALO:HALO + TILE, :] = a[:, c * LANES:(c + 1) * LANES]
    za_scr[...] = _silu(proj(2))
    u_scr[...] = _gelu(proj(3))
    v_scr[...] = _ln(_gelu(proj(4)), sln_g_ref[...], sln_b_ref[...])
    zb_scr[...] = _silu(proj(5))

    for r in range(0, TILE, ROWS):
        slabs = []
        for c in range(N_CONV_SLABS):
            cols = slice(c * LANES, (c + 1) * LANES)
            acc = jnp.broadcast_to(conv_b_ref[:, cols], (ROWS, LANES))
            for k in range(CONV_WIDTH):
                off = r + HALO - (CONV_WIDTH - 1) + k
                acc = acc + conv_w_ref[0, k:k + 1, cols] * a_scr[c, off:off + ROWS, :]
            slabs.append(acc)
        conv = jnp.concatenate(slabs, axis=-1)
        ya = _silu(_ln(conv, cln_g_ref[...], cln_b_ref[...])) * za_scr[r:r + ROWS, :]
        y_scr[r:r + ROWS, 0:D_CONV] = ya
    a_scr[:, 0:HALO, :] = a_scr[:, TILE:TILE + HALO, :]

    lane = lax.broadcasted_iota(jnp.int32, (CHUNK, 2 * SGU_HEAD_DIM), 1)
    lo = lane < SGU_HEAD_DIM
    for c in range(0, TILE, CHUNK):
        for j in range(N_PAIRS):
            cols = slice(j * 2 * SGU_HEAD_DIM, (j + 1) * 2 * SGU_HEAD_DIM)
            vp = v_scr[c:c + CHUNK, cols]
            zero = jnp.zeros_like(vp)
            rhs = jnp.concatenate([jnp.where(lo, vp, zero), jnp.where(lo, zero, vp)], axis=0)
            sv = jnp.dot(wsm_scr[j], rhs, preferred_element_type=f32) + bsf_scr[:, cols]
            yb = u_scr[c:c + CHUNK, cols] * sv * zb_scr[c:c + CHUNK, cols]
            y_scr[c:c + CHUNK, D_CONV + j * 128:D_CONV + (j + 1) * 128] = yb

    mix = jnp.dot(y_scr[...], w_out_ref[0], preferred_element_type=f32)
    h_scr[...] = _ln(ALPHA * h_scr[...] + mix, pln_g_ref[...], pln_b_ref[...])

    pe = jnp.dot(p_ref[0, 0], w_ple_ref[0], preferred_element_type=f32)
    gate = jax.nn.sigmoid(jnp.dot(h_scr[...], w_gate_ref[0], preferred_element_type=f32)
                          + b_gate_ref[...])
    out_ref[0] = h_scr[...] + gate * pe


def kernel(x, p, ln_emb_g, ln_emb_b, w_in, conv_w, conv_b, conv_ln_g, conv_ln_b, sgu_ln_g, sgu_ln_b, w_s, b_s, w_out, post_ln_g, post_ln_b, w_ple, w_ple_gate, b_ple_gate):
    batch, seq, d_model = x.shape
    assert d_model == D_MODEL and seq % TILE == 0 and p.shape[0] == DEPTH
    f32 = jnp.float32

    def const(arr):
        return pl.BlockSpec(arr.shape, lambda b, s: (0,) * arr.ndim, pipeline_mode=pl.Buffered(1))

    params = (ln_emb_g, ln_emb_b, w_in, conv_w, conv_b, conv_ln_g, conv_ln_b, sgu_ln_g, sgu_ln_b,
              w_s, b_s, w_out, post_ln_g, post_ln_b, w_ple, w_ple_gate, b_ple_gate)
    in_specs = [
        pl.BlockSpec((1, TILE, D_MODEL), lambda b, s: (b, s, 0)),
        pl.BlockSpec((1, 1, TILE, D_PLE), lambda b, s: (0, b, s, 0)),
        *[const(arr) for arr in params],
    ]
    scratch = [
        pltpu.VMEM((TILE, D_MODEL), f32),
        pltpu.VMEM((N_CONV_SLABS, HALO + TILE, LANES), f32),
        pltpu.VMEM((TILE, D_CONV), f32),
        pltpu.VMEM((TILE, D_SGU), f32),
        pltpu.VMEM((TILE, D_SGU), f32),
        pltpu.VMEM((TILE, D_SGU), f32),
        pltpu.VMEM((TILE, D_MODEL), f32),
        pltpu.VMEM((N_PAIRS, CHUNK, 2 * CHUNK), f32),
        pltpu.VMEM((CHUNK, D_SGU), f32),
    ]
    return pl.pallas_call(
        _layer_kernel,
        grid=(batch, seq // TILE),
        in_specs=in_specs,
        out_specs=pl.BlockSpec((1, TILE, D_MODEL), lambda b, s: (b, s, 0)),
        out_shape=jax.ShapeDtypeStruct(x.shape, x.dtype),
        scratch_shapes=scratch,
        compiler_params=pltpu.CompilerParams(
            dimension_semantics=("arbitrary", "arbitrary"),
            vmem_limit_bytes=VMEM_LIMIT_BYTES),
        name="deepnorm_hybrid_layer",
    )(x, p, *params)
```

```python
import jax
import jax.numpy as jnp
from jax import lax
from jax.experimental import pallas as pl
from jax.experimental.pallas import tpu as pltpu

D_MODEL = 1024
D_PLE = 256
D_CONV = 512
D_SGU = 512
CONV_WIDTH = 31
N_SGU_HEADS = 8
SGU_HEAD_DIM = 64
CHUNK = 128
LN_EPS = 1e-5
DEPTH = 1
ALPHA = (2 * DEPTH) ** 0.25

LANES = 128
TILE = 512
ROWS = 64
HALO = 32
N_PAIRS = N_SGU_HEADS // 2
N_CONV_SLABS = D_CONV // LANES
VMEM_LIMIT_BYTES = 56 * 1024 * 1024

_INV_SQRT2 = 0.7071067811865476


def _ln(x, g, b):
    mu = jnp.mean(x, axis=-1, keepdims=True)
    xc = x - mu
    var = jnp.mean(xc * xc, axis=-1, keepdims=True)
    return xc * lax.rsqrt(var + LN_EPS) * g + b


def _silu(x):
    return x * jax.nn.sigmoid(x)


def _gelu(x):
    return 0.5 * x * (1.0 + lax.erf(x * _INV_SQRT2))


def _layer_kernel(x_ref, p_ref, ln_g_ref, ln_b_ref, w_in_ref, conv_w_ref, conv_b_ref,
                  cln_g_ref, cln_b_ref, sln_g_ref, sln_b_ref, ws_ref, bs_ref, w_out_ref,
                  pln_g_ref, pln_b_ref, w_ple_ref, w_gate_ref, b_gate_ref,
                  out_ref,
                  h_scr, a_scr, conv_scr, za_scr, u_scr, v_scr, zb_scr, y_scr, wsm_scr, bsf_scr):
    s = pl.program_id(1)
    f32 = jnp.float32

    @pl.when((pl.program_id(0) == 0) & (s == 0))
    def _():
        row = lax.broadcasted_iota(jnp.int32, (CHUNK, CHUNK), 0)
        col = lax.broadcasted_iota(jnp.int32, (CHUNK, CHUNK), 1)
        keep = col <= row
        for j in range(N_PAIRS):
            wsm_scr[j] = jnp.concatenate(
                [jnp.where(keep, ws_ref[0, 2 * j + n], 0.0) for n in range(2)], axis=1)
        bias_t = bs_ref[0].T
        for hd in range(N_SGU_HEADS):
            bsf_scr[:, hd * SGU_HEAD_DIM:(hd + 1) * SGU_HEAD_DIM] = jnp.broadcast_to(
                bias_t[:, hd:hd + 1], (CHUNK, SGU_HEAD_DIM))

    @pl.when(s == 0)
    def _():
        a_scr[:, 0:HALO, :] = jnp.zeros((N_CONV_SLABS, HALO, LANES), f32)

    for r in range(0, TILE, ROWS):
        h_scr[r:r + ROWS, :] = _ln(x_ref[0, r:r + ROWS, :], ln_g_ref[...], ln_b_ref[...])

    def proj_cols(col0, width):
        return jnp.dot(h_scr[...], w_in_ref[0, :, col0:col0 + width], preferred_element_type=f32)

    def proj(g):
        return proj_cols(g * 512, 512)

    half_cols = D_CONV // 2
    slabs_per_half = N_CONV_SLABS // 2

    def conv_slab(c):
        cols = slice(c * LANES, (c + 1) * LANES)
        for r in range(0, TILE, ROWS):
            acc = jnp.broadcast_to(conv_b_ref[:, cols], (ROWS, LANES))
            for k in range(CONV_WIDTH):
                off = r + HALO - (CONV_WIDTH - 1) + k
                acc = acc + conv_w_ref[0, k:k + 1, cols] * a_scr[c, off:off + ROWS, :]
            conv_scr[r:r + ROWS, cols] = acc

    for n in range(2):
        val = proj_cols(n * half_cols, half_cols)
        gate = proj_cols(D_CONV + n * half_cols, half_cols)
        a = val * jax.nn.sigmoid(gate)
        for j in range(slabs_per_half):
            c = n * slabs_per_half + j
            a_scr[c, HALO:HALO + TILE, :] = a[:, j * LANES:(j + 1) * LANES]
        for j in range(slabs_per_half):
            conv_slab(n * slabs_per_half + j)
    za_scr[...] = _silu(proj(2))
    u_scr[...] = _gelu(proj(3))
    v_scr[...] = _ln(_gelu(proj(4)), sln_g_ref[...], sln_b_ref[...])
    zb_scr[...] = _silu(proj(5))

    for r in range(0, TILE, ROWS):
        ya = (_silu(_ln(conv_scr[r:r + ROWS, :], cln_g_ref[...], cln_b_ref[...]))
              * za_scr[r:r + ROWS, :])
        y_scr[r:r + ROWS, 0:D_CONV] = ya
    a_scr[:, 0:HALO, :] = a_scr[:, TILE:TILE + HALO, :]

    lane = lax.broadcasted_iota(jnp.int32, (CHUNK, 2 * SGU_HEAD_DIM), 1)
    lo = lane < SGU_HEAD_DIM
    for c in range(0, TILE, CHUNK):
        for j in range(N_PAIRS):
            cols = slice(j * 2 * SGU_HEAD_DIM, (j + 1) * 2 * SGU_HEAD_DIM)
            vp = v_scr[c:c + CHUNK, cols]
            zero = jnp.zeros_like(vp)
            rhs = jnp.concatenate([jnp.where(lo, vp, zero), jnp.where(lo, zero, vp)], axis=0)
            sv = jnp.dot(wsm_scr[j], rhs, preferred_element_type=f32) + bsf_scr[:, cols]
            yb = u_scr[c:c + CHUNK, cols] * sv * zb_scr[c:c + CHUNK, cols]
            y_scr[c:c + CHUNK, D_CONV + j * 128:D_CONV + (j + 1) * 128] = yb

    mix = jnp.dot(y_scr[...], w_out_ref[0], preferred_element_type=f32)
    h_scr[...] = _ln(ALPHA * h_scr[...] + mix, pln_g_ref[...], pln_b_ref[...])

    pe = jnp.dot(p_ref[0, 0], w_ple_ref[0], preferred_element_type=f32)
    gate = jax.nn.sigmoid(jnp.dot(h_scr[...], w_gate_ref[0], preferred_element_type=f32)
                          + b_gate_ref[...])
    out_ref[0] = h_scr[...] + gate * pe


def kernel(x, p, ln_emb_g, ln_emb_b, w_in, conv_w, conv_b, conv_ln_g, conv_ln_b, sgu_ln_g, sgu_ln_b, w_s, b_s, w_out, post_ln_g, post_ln_b, w_ple, w_ple_gate, b_ple_gate):
    batch, seq, d_model = x.shape
    assert d_model == D_MODEL and seq % TILE == 0 and p.shape[0] == DEPTH
    f32 = jnp.float32

    def const(arr):
        return pl.BlockSpec(arr.shape, lambda b, s: (0,) * arr.ndim, pipeline_mode=pl.Buffered(1))

    params = (ln_emb_g, ln_emb_b, w_in, conv_w, conv_b, conv_ln_g, conv_ln_b, sgu_ln_g, sgu_ln_b,
              w_s, b_s, w_out, post_ln_g, post_ln_b, w_ple, w_ple_gate, b_ple_gate)
    in_specs = [
        pl.BlockSpec((1, TILE, D_MODEL), lambda b, s: (b, s, 0)),
        pl.BlockSpec((1, 1, TILE, D_PLE), lambda b, s: (0, b, s, 0)),
        *[const(arr) for arr in params],
    ]
    scratch = [
        pltpu.VMEM((TILE, D_MODEL), f32),
        pltpu.VMEM((N_CONV_SLABS, HALO + TILE, LANES), f32),
        pltpu.VMEM((TILE, D_CONV), f32),
        pltpu.VMEM((TILE, D_CONV), f32),
        pltpu.VMEM((TILE, D_SGU), f32),
        pltpu.VMEM((TILE, D_SGU), f32),
        pltpu.VMEM((TILE, D_SGU), f32),
        pltpu.VMEM((TILE, D_MODEL), f32),
        pltpu.VMEM((N_PAIRS, CHUNK, 2 * CHUNK), f32),
        pltpu.VMEM((CHUNK, D_SGU), f32),
    ]
    return pl.pallas_call(
        _layer_kernel,
        grid=(batch, seq // TILE),
        in_specs=in_specs,
        out_specs=pl.BlockSpec((1, TILE, D_MODEL), lambda b, s: (b, s, 0)),
        out_shape=jax.ShapeDtypeStruct(x.shape, x.dtype),
        scratch_shapes=scratch,
        compiler_params=pltpu.CompilerParams(
            dimension_semantics=("arbitrary", "arbitrary"),
            vmem_limit_bytes=VMEM_LIMIT_BYTES),
        name="deepnorm_hybrid_layer",
    )(x, p, *params)
```

```python
import jax
import jax.numpy as jnp
from jax import lax
from jax.experimental import pallas as pl
from jax.experimental.pallas import tpu as pltpu

D_MODEL = 1024
D_PLE = 256
D_CONV = 512
D_SGU = 512
CONV_WIDTH = 31
N_SGU_HEADS = 8
SGU_HEAD_DIM = 64
CHUNK = 128
LN_EPS = 1e-5
DEPTH = 1
ALPHA = (2 * DEPTH) ** 0.25

LANES = 128
TILE = 512
ROWS = 64
HALO = 32
N_PAIRS = N_SGU_HEADS // 2
N_CONV_SLABS = D_CONV // LANES
VMEM_LIMIT_BYTES = 56 * 1024 * 1024

_INV_SQRT2 = 0.7071067811865476


def _ln(x, g, b):
    mu = jnp.mean(x, axis=-1, keepdims=True)
    xc = x - mu
    var = jnp.mean(xc * xc, axis=-1, keepdims=True)
    return xc * lax.rsqrt(var + LN_EPS) * g + b


def _silu(x):
    return x * jax.nn.sigmoid(x)


def _gelu(x):
    return 0.5 * x * (1.0 + lax.erf(x * _INV_SQRT2))


def _layer_kernel(x_ref, p_ref, ln_g_ref, ln_b_ref, w_in_ref, conv_w_ref, conv_b_ref,
                  cln_g_ref, cln_b_ref, sln_g_ref, sln_b_ref, ws_ref, bs_ref, w_out_ref,
                  pln_g_ref, pln_b_ref, w_ple_ref, w_gate_ref, b_gate_ref,
                  out_ref,
                  h_scr, a_scr, conv_scr, za_scr, u_scr, v_scr, zb_scr, ya_scr, yb_scr, h1_scr,
                  wsm_scr, bsf_scr):
    s = pl.program_id(1)
    f32 = jnp.float32

    @pl.when((pl.program_id(0) == 0) & (s == 0))
    def _():
        row = lax.broadcasted_iota(jnp.int32, (CHUNK, CHUNK), 0)
        col = lax.broadcasted_iota(jnp.int32, (CHUNK, CHUNK), 1)
        keep = col <= row
        for j in range(N_PAIRS):
            wsm_scr[j] = jnp.concatenate(
                [jnp.where(keep, ws_ref[0, 2 * j + n], 0.0) for n in range(2)], axis=1)
        bias_t = bs_ref[0].T
        for hd in range(N_SGU_HEADS):
            bsf_scr[:, hd * SGU_HEAD_DIM:(hd + 1) * SGU_HEAD_DIM] = jnp.broadcast_to(
                bias_t[:, hd:hd + 1], (CHUNK, SGU_HEAD_DIM))

    @pl.when(s == 0)
    def _():
        a_scr[:, 0:HALO, :] = jnp.zeros((N_CONV_SLABS, HALO, LANES), f32)

    for r in range(0, TILE, ROWS):
        h_scr[r:r + ROWS, :] = _ln(x_ref[0, r:r + ROWS, :], ln_g_ref[...], ln_b_ref[...])

    def proj_cols(col0, width):
        return jnp.dot(h_scr[...], w_in_ref[0, :, col0:col0 + width], preferred_element_type=f32)

    def proj(g):
        return proj_cols(g * 512, 512)

    half_cols = D_CONV // 2
    slabs_per_half = N_CONV_SLABS // 2

    def conv_slab(c):
        cols = slice(c * LANES, (c + 1) * LANES)
        for r in range(0, TILE, ROWS):
            acc = jnp.broadcast_to(conv_b_ref[:, cols], (ROWS, LANES))
            for k in range(CONV_WIDTH):
                off = r + HALO - (CONV_WIDTH - 1) + k
                acc = acc + conv_w_ref[0, k:k + 1, cols] * a_scr[c, off:off + ROWS, :]
            conv_scr[r:r + ROWS, cols] = acc

    for n in range(2):
        val = proj_cols(n * half_cols, half_cols)
        gate = proj_cols(D_CONV + n * half_cols, half_cols)
        a = val * jax.nn.sigmoid(gate)
        for j in range(slabs_per_half):
            c = n * slabs_per_half + j
            a_scr[c, HALO:HALO + TILE, :] = a[:, j * LANES:(j + 1) * LANES]
        if n == 0:
            for j in range(slabs_per_half):
                conv_slab(j)
    za_scr[...] = _silu(proj(2))
    u_scr[...] = _gelu(proj(3))
    v_scr[...] = _ln(_gelu(proj(4)), sln_g_ref[...], sln_b_ref[...])
    zb_scr[...] = _silu(proj(5))

    lane = lax.broadcasted_iota(jnp.int32, (CHUNK, 2 * SGU_HEAD_DIM), 1)
    lo = lane < SGU_HEAD_DIM
    for c in range(0, TILE, CHUNK):
        for j in range(N_PAIRS):
            cols = slice(j * 2 * SGU_HEAD_DIM, (j + 1) * 2 * SGU_HEAD_DIM)
            vp = v_scr[c:c + CHUNK, cols]
            zero = jnp.zeros_like(vp)
            rhs = jnp.concatenate([jnp.where(lo, vp, zero), jnp.where(lo, zero, vp)], axis=0)
            sv = jnp.dot(wsm_scr[j], rhs, preferred_element_type=f32) + bsf_scr[:, cols]
            yb = u_scr[c:c + CHUNK, cols] * sv * zb_scr[c:c + CHUNK, cols]
            yb_scr[c:c + CHUNK, cols] = yb

    h1_scr[...] = ALPHA * h_scr[...] + jnp.dot(yb_scr[...], w_out_ref[0, D_CONV:, :],
                                               preferred_element_type=f32)

    for j in range(slabs_per_half):
        conv_slab(slabs_per_half + j)

    for r in range(0, TILE, ROWS):
        ya = (_silu(_ln(conv_scr[r:r + ROWS, :], cln_g_ref[...], cln_b_ref[...]))
              * za_scr[r:r + ROWS, :])
        ya_scr[r:r + ROWS, :] = ya
    a_scr[:, 0:HALO, :] = a_scr[:, TILE:TILE + HALO, :]

    mix_a = jnp.dot(ya_scr[...], w_out_ref[0, 0:D_CONV, :], preferred_element_type=f32)
    h_scr[...] = _ln(h1_scr[...] + mix_a, pln_g_ref[...], pln_b_ref[...])

    pe = jnp.dot(p_ref[0, 0], w_ple_ref[0], preferred_element_type=f32)
    gate = jax.nn.sigmoid(jnp.dot(h_scr[...], w_gate_ref[0], preferred_element_type=f32)
                          + b_gate_ref[...])
    out_ref[0] = h_scr[...] + gate * pe


def kernel(x, p, ln_emb_g, ln_emb_b, w_in, conv_w, conv_b, conv_ln_g, conv_ln_b, sgu_ln_g, sgu_ln_b, w_s, b_s, w_out, post_ln_g, post_ln_b, w_ple, w_ple_gate, b_ple_gate):
    batch, seq, d_model = x.shape
    assert d_model == D_MODEL and seq % TILE == 0 and p.shape[0] == DEPTH
    f32 = jnp.float32

    def const(arr):
        return pl.BlockSpec(arr.shape, lambda b, s: (0,) * arr.ndim, pipeline_mode=pl.Buffered(1))

    params = (ln_emb_g, ln_emb_b, w_in, conv_w, conv_b, conv_ln_g, conv_ln_b, sgu_ln_g, sgu_ln_b,
              w_s, b_s, w_out, post_ln_g, post_ln_b, w_ple, w_ple_gate, b_ple_gate)
    in_specs = [
        pl.BlockSpec((1, TILE, D_MODEL), lambda b, s: (b, s, 0)),
        pl.BlockSpec((1, 1, TILE, D_PLE), lambda b, s: (0, b, s, 0)),
        *[const(arr) for arr in params],
    ]
    scratch = [
        pltpu.VMEM((TILE, D_MODEL), f32),
        pltpu.VMEM((N_CONV_SLABS, HALO + TILE, LANES), f32),
        pltpu.VMEM((TILE, D_CONV), f32),
        pltpu.VMEM((TILE, D_CONV), f32),
        pltpu.VMEM((TILE, D_SGU), f32),
        pltpu.VMEM((TILE, D_SGU), f32),
        pltpu.VMEM((TILE, D_SGU), f32),
        pltpu.VMEM((TILE, D_CONV), f32),
        pltpu.VMEM((TILE, D_SGU), f32),
        pltpu.VMEM((TILE, D_MODEL), f32),
        pltpu.VMEM((N_PAIRS, CHUNK, 2 * CHUNK), f32),
        pltpu.VMEM((CHUNK, D_SGU), f32),
    ]
    return pl.pallas_call(
        _layer_kernel,
        grid=(batch, seq // TILE),
        in_specs=in_specs,
        out_specs=pl.BlockSpec((1, TILE, D_MODEL), lambda b, s: (b, s, 0)),
        out_shape=jax.ShapeDtypeStruct(x.shape, x.dtype),
        scratch_shapes=scratch,
        compiler_params=pltpu.CompilerParams(
            dimension_semantics=("arbitrary", "arbitrary"),
            vmem_limit_bytes=VMEM_LIMIT_BYTES),
        name="deepnorm_hybrid_layer",
    )(x, p, *params)
```

```python
import jax
import jax.numpy as jnp
from jax import lax
from jax.experimental import pallas as pl
from jax.experimental.pallas import tpu as pltpu

D_MODEL = 1024
D_PLE = 256
D_CONV = 512
D_SGU = 512
CONV_WIDTH = 31
N_SGU_HEADS = 8
SGU_HEAD_DIM = 64
CHUNK = 128
LN_EPS = 1e-5
DEPTH = 1
ALPHA = (2 * DEPTH) ** 0.25

LANES = 128
MXU_COLS = 256
TILE = 512
ROWS = 64
HALO = 32
N_PAIRS = N_SGU_HEADS // 2
N_CONV_SLABS = D_CONV // LANES
VMEM_LIMIT_BYTES = 60 * 1024 * 1024

_INV_SQRT2 = 0.7071067811865476


def _ln(x, g, b):
    mu = jnp.mean(x, axis=-1, keepdims=True)
    xc = x - mu
    var = jnp.mean(xc * xc, axis=-1, keepdims=True)
    return xc * lax.rsqrt(var + LN_EPS) * g + b


def _silu(x):
    return x * jax.nn.sigmoid(x)


def _gelu(x):
    return 0.5 * x * (1.0 + lax.erf(x * _INV_SQRT2))


def _mm(lhs_bf16, w_bf16):
    return jnp.dot(lhs_bf16, w_bf16, preferred_element_type=jnp.float32)


def _layer_kernel(x_ref, p_ref, ln_g_ref, ln_b_ref, w_in_ref, conv_w_ref, conv_b_ref,
                  cln_g_ref, cln_b_ref, sln_g_ref, sln_b_ref, ws_ref, bs_ref, w_out_ref,
                  pln_g_ref, pln_b_ref, w_ple_ref, w_gate_ref, b_gate_ref,
                  out_ref,
                  h_scr, hb_scr, a_scr, conv_scr, za_scr, u_scr, v_scr, zb_scr, ya_scr, yb_scr,
                  h1_scr, wsm_scr, bsf_scr, w_in_scr, w_out_scr, w_ple_scr, w_gate_scr):
    s = pl.program_id(1)
    f32 = jnp.float32
    bf16 = jnp.bfloat16

    @pl.when((pl.program_id(0) == 0) & (s == 0))
    def _():
        row = lax.broadcasted_iota(jnp.int32, (CHUNK, CHUNK), 0)
        col = lax.broadcasted_iota(jnp.int32, (CHUNK, CHUNK), 1)
        keep = col <= row
        for j in range(N_PAIRS):
            wsm_scr[j] = jnp.concatenate(
                [jnp.where(keep, ws_ref[0, 2 * j + n], 0.0) for n in range(2)], axis=1)
        bias_t = bs_ref[0].T
        for hd in range(N_SGU_HEADS):
            bsf_scr[:, hd * SGU_HEAD_DIM:(hd + 1) * SGU_HEAD_DIM] = jnp.broadcast_to(
                bias_t[:, hd:hd + 1], (CHUNK, SGU_HEAD_DIM))
        for dst, src in ((w_in_scr, w_in_ref), (w_out_scr, w_out_ref), (w_ple_scr, w_ple_ref),
                         (w_gate_scr, w_gate_ref)):
            for c in range(0, dst.shape[1], MXU_COLS):
                dst[:, c:c + MXU_COLS] = src[0, :, c:c + MXU_COLS].astype(bf16)

    @pl.when(s == 0)
    def _():
        a_scr[:, 0:HALO, :] = jnp.zeros((N_CONV_SLABS, HALO, LANES), f32)

    for r in range(0, TILE, ROWS):
        h = _ln(x_ref[0, r:r + ROWS, :], ln_g_ref[...], ln_b_ref[...])
        h_scr[r:r + ROWS, :] = h
        hb_scr[r:r + ROWS, :] = h.astype(bf16)

    def proj_cols(col0, width):
        return _mm(hb_scr[...], w_in_scr[:, col0:col0 + width])

    def proj(g):
        return proj_cols(g * 512, 512)

    half_cols = D_CONV // 2
    slabs_per_half = N_CONV_SLABS // 2

    def conv_slab(c):
        cols = slice(c * LANES, (c + 1) * LANES)
        for r in range(0, TILE, ROWS):
            acc = jnp.broadcast_to(conv_b_ref[:, cols], (ROWS, LANES))
            for k in range(CONV_WIDTH):
                off = r + HALO - (CONV_WIDTH - 1) + k
                acc = acc + conv_w_ref[0, k:k + 1, cols] * a_scr[c, off:off + ROWS, :]
            conv_scr[r:r + ROWS, cols] = acc

    for n in range(2):
        val = proj_cols(n * half_cols, half_cols)
        gate = proj_cols(D_CONV + n * half_cols, half_cols)
        a = val * jax.nn.sigmoid(gate)
        for j in range(slabs_per_half):
            c = n * slabs_per_half + j
            a_scr[c, HALO:HALO + TILE, :] = a[:, j * LANES:(j + 1) * LANES]
        if n == 0:
            for j in range(slabs_per_half):
                conv_slab(j)
    za_scr[...] = _silu(proj(2))
    u_scr[...] = _gelu(proj(3))
    v_scr[...] = _ln(_gelu(proj(4)), sln_g_ref[...], sln_b_ref[...])
    zb_scr[...] = _silu(proj(5))

    lane = lax.broadcasted_iota(jnp.int32, (CHUNK, 2 * SGU_HEAD_DIM), 1)
    lo = lane < SGU_HEAD_DIM
    for c in range(0, TILE, CHUNK):
        for j in range(N_PAIRS):
            cols = slice(j * 2 * SGU_HEAD_DIM, (j + 1) * 2 * SGU_HEAD_DIM)
            vp = v_scr[c:c + CHUNK, cols]
            zero = jnp.zeros_like(vp)
            rhs = jnp.concatenate([jnp.where(lo, vp, zero), jnp.where(lo, zero, vp)], axis=0)
            sv = jnp.dot(wsm_scr[j], rhs, preferred_element_type=f32) + bsf_scr[:, cols]
            yb = u_scr[c:c + CHUNK, cols] * sv * zb_scr[c:c + CHUNK, cols]
            yb_scr[c:c + CHUNK, cols] = yb.astype(bf16)

    h1_scr[...] = ALPHA * h_scr[...] + _mm(yb_scr[...], w_out_scr[D_CONV:, :])

    for j in range(slabs_per_half):
        conv_slab(slabs_per_half + j)

    for r in range(0, TILE, ROWS):
        ya = (_silu(_ln(conv_scr[r:r + ROWS, :], cln_g_ref[...], cln_b_ref[...]))
              * za_scr[r:r + ROWS, :])
        ya_scr[r:r + ROWS, :] = ya.astype(bf16)
    a_scr[:, 0:HALO, :] = a_scr[:, TILE:TILE + HALO, :]

    mix_a = _mm(ya_scr[...], w_out_scr[0:D_CONV, :])
    hpost = _ln(h1_scr[...] + mix_a, pln_g_ref[...], pln_b_ref[...])
    h_scr[...] = hpost
    hb_scr[...] = hpost.astype(bf16)

    pe = _mm(p_ref[0, 0].astype(bf16), w_ple_scr[...])
    gate = jax.nn.sigmoid(_mm(hb_scr[...], w_gate_scr[...]) + b_gate_ref[...])
    out_ref[0] = h_scr[...] + gate * pe


def kernel(x, p, ln_emb_g, ln_emb_b, w_in, conv_w, conv_b, conv_ln_g, conv_ln_b, sgu_ln_g, sgu_ln_b, w_s, b_s, w_out, post_ln_g, post_ln_b, w_ple, w_ple_gate, b_ple_gate):
    batch, seq, d_model = x.shape
    assert d_model == D_MODEL and seq % TILE == 0 and p.shape[0] == DEPTH
    f32 = jnp.float32
    bf16 = jnp.bfloat16

    def const(arr):
        return pl.BlockSpec(arr.shape, lambda b, s: (0,) * arr.ndim, pipeline_mode=pl.Buffered(1))

    params = (ln_emb_g, ln_emb_b, w_in, conv_w, conv_b, conv_ln_g, conv_ln_b, sgu_ln_g, sgu_ln_b,
              w_s, b_s, w_out, post_ln_g, post_ln_b, w_ple, w_ple_gate, b_ple_gate)
    in_specs = [
        pl.BlockSpec((1, TILE, D_MODEL), lambda b, s: (b, s, 0)),
        pl.BlockSpec((1, 1, TILE, D_PLE), lambda b, s: (0, b, s, 0)),
        *[const(arr) for arr in params],
    ]
    scratch = [
        pltpu.VMEM((TILE, D_MODEL), f32),
        pltpu.VMEM((TILE, D_MODEL), bf16),
        pltpu.VMEM((N_CONV_SLABS, HALO + TILE, LANES), f32),
        pltpu.VMEM((TILE, D_CONV), f32),
        pltpu.VMEM((TILE, D_CONV), f32),
        pltpu.VMEM((TILE, D_SGU), f32),
        pltpu.VMEM((TILE, D_SGU), f32),
        pltpu.VMEM((TILE, D_SGU), f32),
        pltpu.VMEM((TILE, D_CONV), bf16),
        pltpu.VMEM((TILE, D_SGU), bf16),
        pltpu.VMEM((TILE, D_MODEL), f32),
        pltpu.VMEM((N_PAIRS, CHUNK, 2 * CHUNK), f32),
        pltpu.VMEM((CHUNK, D_SGU), f32),
        pltpu.VMEM(w_in.shape[1:], bf16),
        pltpu.VMEM(w_out.shape[1:], bf16),
        pltpu.VMEM(w_ple.shape[1:], bf16),
        pltpu.VMEM(w_ple_gate.shape[1:], bf16),
    ]
    return pl.pallas_call(
        _layer_kernel,
        grid=(batch, seq // TILE),
        in_specs=in_specs,
        out_specs=pl.BlockSpec((1, TILE, D_MODEL), lambda b, s: (b, s, 0)),
        out_shape=jax.ShapeDtypeStruct(x.shape, x.dtype),
        scratch_shapes=scratch,
        compiler_params=pltpu.CompilerParams(
            dimension_semantics=("arbitrary", "arbitrary"),
            vmem_limit_bytes=VMEM_LIMIT_BYTES),
        name="deepnorm_hybrid_layer",
    )(x, p, *params)
```

```python
import jax
import jax.numpy as jnp
from jax import lax
from jax.experimental import pallas as pl
from jax.experimental.pallas import tpu as pltpu

D_MODEL = 1024
D_PLE = 256
D_CONV = 512
D_SGU = 512
CONV_WIDTH = 31
N_SGU_HEADS = 8
SGU_HEAD_DIM = 64
CHUNK = 128
LN_EPS = 1e-5
DEPTH = 1
ALPHA = (2 * DEPTH) ** 0.25

LANES = 128
MXU_COLS = 256
TILE = 512
ROWS = 64
HALO = 32
N_PAIRS = N_SGU_HEADS // 2
N_CONV_SLABS = D_CONV // LANES
VMEM_LIMIT_BYTES = 60 * 1024 * 1024

_INV_SQRT2 = 0.7071067811865476


def _ln(x, g, b):
    mu = jnp.mean(x, axis=-1, keepdims=True)
    xc = x - mu
    var = jnp.mean(xc * xc, axis=-1, keepdims=True)
    return xc * lax.rsqrt(var + LN_EPS) * g + b


def _silu(x):
    return x * jax.nn.sigmoid(x)


def _gelu(x):
    return 0.5 * x * (1.0 + lax.erf(x * _INV_SQRT2))


def _mm(lhs_f32, w_bf16):
    return lax.dot_general(lhs_f32, w_bf16, (((1,), (0,)), ((), ())),
                           preferred_element_type=jnp.float32)


def _layer_kernel(x_ref, p_ref, ln_g_ref, ln_b_ref, w_in_ref, conv_w_ref, conv_b_ref,
                  cln_g_ref, cln_b_ref, sln_g_ref, sln_b_ref, ws_ref, bs_ref, w_out_ref,
                  pln_g_ref, pln_b_ref, w_ple_ref, w_gate_ref, b_gate_ref,
                  out_ref,
                  h_scr, a_scr, conv_scr, za_scr, u_scr, v_scr, zb_scr, ya_scr, yb_scr, h1_scr,
                  wsm_scr, bsf_scr, w_in_scr, w_out_scr, w_ple_scr, w_gate_scr):
    s = pl.program_id(1)
    f32 = jnp.float32

    @pl.when((pl.program_id(0) == 0) & (s == 0))
    def _():
        row = lax.broadcasted_iota(jnp.int32, (CHUNK, CHUNK), 0)
        col = lax.broadcasted_iota(jnp.int32, (CHUNK, CHUNK), 1)
        keep = col <= row
        for j in range(N_PAIRS):
            wsm_scr[j] = jnp.concatenate(
                [jnp.where(keep, ws_ref[0, 2 * j + n], 0.0) for n in range(2)], axis=1)
        bias_t = bs_ref[0].T
        for hd in range(N_SGU_HEADS):
            bsf_scr[:, hd * SGU_HEAD_DIM:(hd + 1) * SGU_HEAD_DIM] = jnp.broadcast_to(
                bias_t[:, hd:hd + 1], (CHUNK, SGU_HEAD_DIM))
        for dst, src in ((w_in_scr, w_in_ref), (w_out_scr, w_out_ref), (w_ple_scr, w_ple_ref),
                         (w_gate_scr, w_gate_ref)):
            for c in range(0, dst.shape[1], MXU_COLS):
                dst[:, c:c + MXU_COLS] = src[0, :, c:c + MXU_COLS].astype(jnp.bfloat16)

    @pl.when(s == 0)
    def _():
        a_scr[:, 0:HALO, :] = jnp.zeros((N_CONV_SLABS, HALO, LANES), f32)

    for r in range(0, TILE, ROWS):
        h_scr[r:r + ROWS, :] = _ln(x_ref[0, r:r + ROWS, :], ln_g_ref[...], ln_b_ref[...])

    def proj_cols(col0, width):
        return _mm(h_scr[...], w_in_scr[:, col0:col0 + width])

    def proj(g):
        return proj_cols(g * 512, 512)

    half_cols = D_CONV // 2
    slabs_per_half = N_CONV_SLABS // 2

    def conv_slab(c):
        cols = slice(c * LANES, (c + 1) * LANES)
        for r in range(0, TILE, ROWS):
            acc = jnp.broadcast_to(conv_b_ref[:, cols], (ROWS, LANES))
            for k in range(CONV_WIDTH):
                off = r + HALO - (CONV_WIDTH - 1) + k
                acc = acc + conv_w_ref[0, k:k + 1, cols] * a_scr[c, off:off + ROWS, :]
            conv_scr[r:r + ROWS, cols] = acc

    for n in range(2):
        val = proj_cols(n * half_cols, half_cols)
        gate = proj_cols(D_CONV + n * half_cols, half_cols)
        a = val * jax.nn.sigmoid(gate)
        for j in range(slabs_per_half):
            c = n * slabs_per_half + j
            a_scr[c, HALO:HALO + TILE, :] = a[:, j * LANES:(j + 1) * LANES]
        if n == 0:
            for j in range(slabs_per_half):
                conv_slab(j)
    za_scr[...] = _silu(proj(2))
    u_scr[...] = _gelu(proj(3))
    v_scr[...] = _ln(_gelu(proj(4)), sln_g_ref[...], sln_b_ref[...])
    zb_scr[...] = _silu(proj(5))

    lane = lax.broadcasted_iota(jnp.int32, (CHUNK, 2 * SGU_HEAD_DIM), 1)
    lo = lane < SGU_HEAD_DIM
    for c in range(0, TILE, CHUNK):
        for j in range(N_PAIRS):
            cols = slice(j * 2 * SGU_HEAD_DIM, (j + 1) * 2 * SGU_HEAD_DIM)
            vp = v_scr[c:c + CHUNK, cols]
            zero = jnp.zeros_like(vp)
            rhs = jnp.concatenate([jnp.where(lo, vp, zero), jnp.where(lo, zero, vp)], axis=0)
            sv = jnp.dot(wsm_scr[j], rhs, preferred_element_type=f32) + bsf_scr[:, cols]
            yb = u_scr[c:c + CHUNK, cols] * sv * zb_scr[c:c + CHUNK, cols]
            yb_scr[c:c + CHUNK, cols] = yb

    h1_scr[...] = ALPHA * h_scr[...] + _mm(yb_scr[...], w_out_scr[D_CONV:, :])

    for j in range(slabs_per_half):
        conv_slab(slabs_per_half + j)

    for r in range(0, TILE, ROWS):
        ya = (_silu(_ln(conv_scr[r:r + ROWS, :], cln_g_ref[...], cln_b_ref[...]))
              * za_scr[r:r + ROWS, :])
        ya_scr[r:r + ROWS, :] = ya
    a_scr[:, 0:HALO, :] = a_scr[:, TILE:TILE + HALO, :]

    mix_a = _mm(ya_scr[...], w_out_scr[0:D_CONV, :])
    h_scr[...] = _ln(h1_scr[...] + mix_a, pln_g_ref[...], pln_b_ref[...])

    pe = _mm(p_ref[0, 0], w_ple_scr[...])
    gate = jax.nn.sigmoid(_mm(h_scr[...], w_gate_scr[...]) + b_gate_ref[...])
    out_ref[0] = h_scr[...] + gate * pe


def kernel(x, p, ln_emb_g, ln_emb_b, w_in, conv_w, conv_b, conv_ln_g, conv_ln_b, sgu_ln_g, sgu_ln_b, w_s, b_s, w_out, post_ln_g, post_ln_b, w_ple, w_ple_gate, b_ple_gate):
    batch, seq, d_model = x.shape
    assert d_model == D_MODEL and seq % TILE == 0 and p.shape[0] == DEPTH
    f32 = jnp.float32
    bf16 = jnp.bfloat16

    def const(arr):
        return pl.BlockSpec(arr.shape, lambda b, s: (0,) * arr.ndim, pipeline_mode=pl.Buffered(1))

    params = (ln_emb_g, ln_emb_b, w_in, conv_w, conv_b, conv_ln_g, conv_ln_b, sgu_ln_g, sgu_ln_b,
              w_s, b_s, w_out, post_ln_g, post_ln_b, w_ple, w_ple_gate, b_ple_gate)
    in_specs = [
        pl.BlockSpec((1, TILE, D_MODEL), lambda b, s: (b, s, 0)),
        pl.BlockSpec((1, 1, TILE, D_PLE), lambda b, s: (0, b, s, 0)),
        *[const(arr) for arr in params],
    ]
    scratch = [
        pltpu.VMEM((TILE, D_MODEL), f32),
        pltpu.VMEM((N_CONV_SLABS, HALO + TILE, LANES), f32),
        pltpu.VMEM((TILE, D_CONV), f32),
        pltpu.VMEM((TILE, D_CONV), f32),
        pltpu.VMEM((TILE, D_SGU), f32),
        pltpu.VMEM((TILE, D_SGU), f32),
        pltpu.VMEM((TILE, D_SGU), f32),
        pltpu.VMEM((TILE, D_CONV), f32),
        pltpu.VMEM((TILE, D_SGU), f32),
        pltpu.VMEM((TILE, D_MODEL), f32),
        pltpu.VMEM((N_PAIRS, CHUNK, 2 * CHUNK), f32),
        pltpu.VMEM((CHUNK, D_SGU), f32),
        pltpu.VMEM(w_in.shape[1:], bf16),
        pltpu.VMEM(w_out.shape[1:], bf16),
        pltpu.VMEM(w_ple.shape[1:], bf16),
        pltpu.VMEM(w_ple_gate.shape[1:], bf16),
    ]
    return pl.pallas_call(
        _layer_kernel,
        grid=(batch, seq // TILE),
        in_specs=in_specs,
        out_specs=pl.BlockSpec((1, TILE, D_MODEL), lambda b, s: (b, s, 0)),
        out_shape=jax.ShapeDtypeStruct(x.shape, x.dtype),
        scratch_shapes=scratch,
        compiler_params=pltpu.CompilerParams(
            dimension_semantics=("arbitrary", "arbitrary"),
            vmem_limit_bytes=VMEM_LIMIT_BYTES),
        name="deepnorm_hybrid_layer",
    )(x, p, *params)
```

```python
import jax
import jax.numpy as jnp
from jax import lax
from jax.experimental import pallas as pl
from jax.experimental.pallas import tpu as pltpu

D_MODEL = 1024
D_PLE = 256
D_CONV = 512
D_SGU = 512
CONV_WIDTH = 31
N_SGU_HEADS = 8
SGU_HEAD_DIM = 64
CHUNK = 128
LN_EPS = 1e-5
DEPTH = 1
ALPHA = (2 * DEPTH) ** 0.25

LANES = 128
MXU_COLS = 256
TILE = 512
ROWS = 64
N_TAIL_BLOCKS = 2
HALO = 32
N_PAIRS = N_SGU_HEADS // 2
N_CONV_SLABS = D_CONV // LANES
VMEM_LIMIT_BYTES = 60 * 1024 * 1024

_INV_SQRT2 = 0.7071067811865476


def _ln(x, g, b):
    mu = jnp.mean(x, axis=-1, keepdims=True)
    xc = x - mu
    var = jnp.mean(xc * xc, axis=-1, keepdims=True)
    return xc * lax.rsqrt(var + LN_EPS) * g + b


def _silu(x):
    return x * jax.nn.sigmoid(x)


def _gelu(x):
    return 0.5 * x * (1.0 + lax.erf(x * _INV_SQRT2))


def _mm(lhs_f32, w_bf16):
    return lax.dot_general(lhs_f32, w_bf16, (((1,), (0,)), ((), ())),
                           preferred_element_type=jnp.float32)


def _layer_kernel(x_ref, p_ref, ln_g_ref, ln_b_ref, w_in_ref, conv_w_ref, conv_b_ref,
                  cln_g_ref, cln_b_ref, sln_g_ref, sln_b_ref, ws_ref, bs_ref, w_out_ref,
                  pln_g_ref, pln_b_ref, w_ple_ref, w_gate_ref, b_gate_ref,
                  out_ref,
                  h_scr, a_scr, conv_scr, za_scr, u_scr, v_scr, zb_scr, ya_scr, yb_scr, h1_scr,
                  wsm_scr, bsf_scr, w_in_scr, w_out_scr, w_ple_scr, w_gate_scr):
    s = pl.program_id(1)
    f32 = jnp.float32

    @pl.when((pl.program_id(0) == 0) & (s == 0))
    def _():
        row = lax.broadcasted_iota(jnp.int32, (CHUNK, CHUNK), 0)
        col = lax.broadcasted_iota(jnp.int32, (CHUNK, CHUNK), 1)
        keep = col <= row
        for j in range(N_PAIRS):
            wsm_scr[j] = jnp.concatenate(
                [jnp.where(keep, ws_ref[0, 2 * j + n], 0.0) for n in range(2)], axis=1)
        bias_t = bs_ref[0].T
        for hd in range(N_SGU_HEADS):
            bsf_scr[:, hd * SGU_HEAD_DIM:(hd + 1) * SGU_HEAD_DIM] = jnp.broadcast_to(
                bias_t[:, hd:hd + 1], (CHUNK, SGU_HEAD_DIM))
        for dst, src in ((w_in_scr, w_in_ref), (w_out_scr, w_out_ref), (w_ple_scr, w_ple_ref),
                         (w_gate_scr, w_gate_ref)):
            for c in range(0, dst.shape[1], MXU_COLS):
                dst[:, c:c + MXU_COLS] = src[0, :, c:c + MXU_COLS].astype(jnp.bfloat16)

    @pl.when(s == 0)
    def _():
        a_scr[:, 0:HALO, :] = jnp.zeros((N_CONV_SLABS, HALO, LANES), f32)

    half_cols = D_CONV // 2
    slabs_per_half = N_CONV_SLABS // 2

    def input_ln(r0, n):
        for r in range(r0, r0 + n, ROWS):
            h_scr[r:r + ROWS, :] = _ln(x_ref[0, r:r + ROWS, :], ln_g_ref[...], ln_b_ref[...])

    def proj(r0, n, col0, width):
        return _mm(h_scr[r0:r0 + n, :], w_in_scr[:, col0:col0 + width])

    def glu(half, r0, n):
        val = proj(r0, n, half * half_cols, half_cols)
        gate = proj(r0, n, D_CONV + half * half_cols, half_cols)
        a = val * jax.nn.sigmoid(gate)
        for j in range(slabs_per_half):
            c = half * slabs_per_half + j
            a_scr[c, HALO + r0:HALO + r0 + n, :] = a[:, j * LANES:(j + 1) * LANES]

    def conv_half(half, r0, n):
        for c in range(half * slabs_per_half, (half + 1) * slabs_per_half):
            cols = slice(c * LANES, (c + 1) * LANES)
            for r in range(r0, r0 + n, ROWS):
                acc = jnp.broadcast_to(conv_b_ref[:, cols], (ROWS, LANES))
                for k in range(CONV_WIDTH):
                    off = r + HALO - (CONV_WIDTH - 1) + k
                    acc = acc + conv_w_ref[0, k:k + 1, cols] * a_scr[c, off:off + ROWS, :]
                conv_scr[r:r + ROWS, cols] = acc

    def gates(r0, n):
        rows = slice(r0, r0 + n)
        za_scr[rows, :] = _silu(proj(r0, n, 2 * 512, 512))
        u_scr[rows, :] = _gelu(proj(r0, n, 3 * 512, 512))
        v_scr[rows, :] = _ln(_gelu(proj(r0, n, 4 * 512, 512)), sln_g_ref[...], sln_b_ref[...])
        zb_scr[rows, :] = _silu(proj(r0, n, 5 * 512, 512))

    lane = lax.broadcasted_iota(jnp.int32, (CHUNK, 2 * SGU_HEAD_DIM), 1)
    lo = lane < SGU_HEAD_DIM

    def spatial_gating(r0, n):
        for c in range(r0, r0 + n, CHUNK):
            for j in range(N_PAIRS):
                cols = slice(j * 2 * SGU_HEAD_DIM, (j + 1) * 2 * SGU_HEAD_DIM)
                vp = v_scr[c:c + CHUNK, cols]
                zero = jnp.zeros_like(vp)
                rhs = jnp.concatenate([jnp.where(lo, vp, zero), jnp.where(lo, zero, vp)], axis=0)
                sv = jnp.dot(wsm_scr[j], rhs, preferred_element_type=f32) + bsf_scr[:, cols]
                yb = u_scr[c:c + CHUNK, cols] * sv * zb_scr[c:c + CHUNK, cols]
                yb_scr[c:c + CHUNK, cols] = yb
        rows = slice(r0, r0 + n)
        h1_scr[rows, :] = ALPHA * h_scr[rows, :] + _mm(yb_scr[rows, :], w_out_scr[D_CONV:, :])

    def conv_out(r0, n):
        for r in range(r0, r0 + n, ROWS):
            ya = (_silu(_ln(conv_scr[r:r + ROWS, :], cln_g_ref[...], cln_b_ref[...]))
                  * za_scr[r:r + ROWS, :])
            ya_scr[r:r + ROWS, :] = ya

    def mix_post(r0, n):
        rows = slice(r0, r0 + n)
        mix_a = _mm(ya_scr[rows, :], w_out_scr[0:D_CONV, :])
        h_scr[rows, :] = _ln(h1_scr[rows, :] + mix_a, pln_g_ref[...], pln_b_ref[...])

    def emit(r0, n):
        rows = slice(r0, r0 + n)
        pe = _mm(p_ref[0, 0, rows, :], w_ple_scr[...])
        gate = jax.nn.sigmoid(_mm(h_scr[rows, :], w_gate_scr[...]) + b_gate_ref[...])
        out_ref[0, rows, :] = h_scr[rows, :] + gate * pe

    tile = (0, TILE)
    input_ln(*tile)
    glu(0, *tile)
    conv_half(0, *tile)
    glu(1, *tile)
    gates(*tile)
    spatial_gating(*tile)
    conv_half(1, *tile)
    conv_out(*tile)
    a_scr[:, 0:HALO, :] = a_scr[:, TILE:TILE + HALO, :]
    blocks = [(r0, TILE // N_TAIL_BLOCKS) for r0 in range(0, TILE, TILE // N_TAIL_BLOCKS)]
    for blk in blocks:
        mix_post(*blk)
    for blk in blocks:
        emit(*blk)


def kernel(x, p, ln_emb_g, ln_emb_b, w_in, conv_w, conv_b, conv_ln_g, conv_ln_b, sgu_ln_g, sgu_ln_b, w_s, b_s, w_out, post_ln_g, post_ln_b, w_ple, w_ple_gate, b_ple_gate):
    batch, seq, d_model = x.shape
    assert d_model == D_MODEL and seq % TILE == 0 and p.shape[0] == DEPTH
    f32 = jnp.float32
    bf16 = jnp.bfloat16

    def const(arr):
        return pl.BlockSpec(arr.shape, lambda b, s: (0,) * arr.ndim, pipeline_mode=pl.Buffered(1))

    params = (ln_emb_g, ln_emb_b, w_in, conv_w, conv_b, conv_ln_g, conv_ln_b, sgu_ln_g, sgu_ln_b,
              w_s, b_s, w_out, post_ln_g, post_ln_b, w_ple, w_ple_gate, b_ple_gate)
    in_specs = [
        pl.BlockSpec((1, TILE, D_MODEL), lambda b, s: (b, s, 0)),
        pl.BlockSpec((1, 1, TILE, D_PLE), lambda b, s: (0, b, s, 0)),
        *[const(arr) for arr in params],
    ]
    scratch = [
        pltpu.VMEM((TILE, D_MODEL), f32),
        pltpu.VMEM((N_CONV_SLABS, HALO + TILE, LANES), f32),
        pltpu.VMEM((TILE, D_CONV), f32),
        pltpu.VMEM((TILE, D_CONV), f32),
        pltpu.VMEM((TILE, D_SGU), f32),
        pltpu.VMEM((TILE, D_SGU), f32),
        pltpu.VMEM((TILE, D_SGU), f32),
        pltpu.VMEM((TILE, D_CONV), f32),
        pltpu.VMEM((TILE, D_SGU), f32),
        pltpu.VMEM((TILE, D_MODEL), f32),
        pltpu.VMEM((N_PAIRS, CHUNK, 2 * CHUNK), f32),
        pltpu.VMEM((CHUNK, D_SGU), f32),
        pltpu.VMEM(w_in.shape[1:], bf16),
        pltpu.VMEM(w_out.shape[1:], bf16),
        pltpu.VMEM(w_ple.shape[1:], bf16),
        pltpu.VMEM(w_ple_gate.shape[1:], bf16),
    ]
    return pl.pallas_call(
        _layer_kernel,
        grid=(batch, seq // TILE),
        in_specs=in_specs,
        out_specs=pl.BlockSpec((1, TILE, D_MODEL), lambda b, s: (b, s, 0)),
        out_shape=jax.ShapeDtypeStruct(x.shape, x.dtype),
        scratch_shapes=scratch,
        compiler_params=pltpu.CompilerParams(
            dimension_semantics=("arbitrary", "arbitrary"),
            vmem_limit_bytes=VMEM_LIMIT_BYTES),
        name="deepnorm_hybrid_layer",
    )(x, p, *params)
```

```python
import jax
import jax.numpy as jnp
from jax import lax
from jax.experimental import pallas as pl
from jax.experimental.pallas import tpu as pltpu

D_MODEL = 1024
D_PLE = 256
D_CONV = 512
D_SGU = 512
CONV_WIDTH = 31
N_SGU_HEADS = 8
SGU_HEAD_DIM = 64
CHUNK = 128
LN_EPS = 1e-5
DEPTH = 1
ALPHA = (2 * DEPTH) ** 0.25

LANES = 128
MXU_COLS = 256
TILE = 512
ROWS = 64
LN_ROWS = 16
HALO = 32
N_PAIRS = N_SGU_HEADS // 2
N_CONV_SLABS = D_CONV // LANES
VMEM_LIMIT_BYTES = 60 * 1024 * 1024

_INV_SQRT2 = 0.7071067811865476


def _ln(x, g, b):
    mu = jnp.mean(x, axis=-1, keepdims=True)
    xc = x - mu
    var = jnp.mean(xc * xc, axis=-1, keepdims=True)
    return xc * lax.rsqrt(var + LN_EPS) * g + b


def _silu(x):
    return x * jax.nn.sigmoid(x)


def _gelu(x):
    return 0.5 * x * (1.0 + lax.erf(x * _INV_SQRT2))


def _mm(lhs_f32, w_bf16):
    return lax.dot_general(lhs_f32, w_bf16, (((1,), (0,)), ((), ())),
                           preferred_element_type=jnp.float32)


def _layer_kernel(x_ref, p_ref, ln_g_ref, ln_b_ref, w_in_ref, conv_w_ref, conv_b_ref,
                  cln_g_ref, cln_b_ref, sln_g_ref, sln_b_ref, ws_ref, bs_ref, w_out_ref,
                  pln_g_ref, pln_b_ref, w_ple_ref, w_gate_ref, b_gate_ref,
                  out_ref,
                  h_scr, a_scr, conv_scr, za_scr, u_scr, v_scr, zb_scr, ya_scr, yb_scr, h1_scr,
                  wsm_scr, bsf_scr, w_in_scr, w_out_scr, w_ple_scr, w_gate_scr):
    s = pl.program_id(1)
    f32 = jnp.float32

    @pl.when((pl.program_id(0) == 0) & (s == 0))
    def _():
        row = lax.broadcasted_iota(jnp.int32, (CHUNK, CHUNK), 0)
        col = lax.broadcasted_iota(jnp.int32, (CHUNK, CHUNK), 1)
        keep = col <= row
        for j in range(N_PAIRS):
            wsm_scr[j] = jnp.concatenate(
                [jnp.where(keep, ws_ref[0, 2 * j + n], 0.0) for n in range(2)], axis=1)
        bias_t = bs_ref[0].T
        for hd in range(N_SGU_HEADS):
            bsf_scr[:, hd * SGU_HEAD_DIM:(hd + 1) * SGU_HEAD_DIM] = jnp.broadcast_to(
                bias_t[:, hd:hd + 1], (CHUNK, SGU_HEAD_DIM))
        for dst, src in ((w_in_scr, w_in_ref), (w_out_scr, w_out_ref), (w_ple_scr, w_ple_ref),
                         (w_gate_scr, w_gate_ref)):
            for c in range(0, dst.shape[1], MXU_COLS):
                dst[:, c:c + MXU_COLS] = src[0, :, c:c + MXU_COLS].astype(jnp.bfloat16)

    @pl.when(s == 0)
    def _():
        a_scr[:, 0:HALO, :] = jnp.zeros((N_CONV_SLABS, HALO, LANES), f32)

    half_cols = D_CONV // 2
    slabs_per_half = N_CONV_SLABS // 2

    def input_ln(r0, n):
        for r in range(r0, r0 + n, LN_ROWS):
            h_scr[r:r + LN_ROWS, :] = _ln(x_ref[0, r:r + LN_ROWS, :], ln_g_ref[...], ln_b_ref[...])

    def proj(r0, n, col0, width):
        return _mm(h_scr[r0:r0 + n, :], w_in_scr[:, col0:col0 + width])

    def glu(half, r0, n):
        val = proj(r0, n, half * half_cols, half_cols)
        gate = proj(r0, n, D_CONV + half * half_cols, half_cols)
        a = val * jax.nn.sigmoid(gate)
        for j in range(slabs_per_half):
            c = half * slabs_per_half + j
            a_scr[c, HALO + r0:HALO + r0 + n, :] = a[:, j * LANES:(j + 1) * LANES]

    def conv_half(half, r0, n):
        for c in range(half * slabs_per_half, (half + 1) * slabs_per_half):
            cols = slice(c * LANES, (c + 1) * LANES)
            for r in range(r0, r0 + n, ROWS):
                acc = jnp.broadcast_to(conv_b_ref[:, cols], (ROWS, LANES))
                for k in range(CONV_WIDTH):
                    off = r + HALO - (CONV_WIDTH - 1) + k
                    acc = acc + conv_w_ref[0, k:k + 1, cols] * a_scr[c, off:off + ROWS, :]
                conv_scr[r:r + ROWS, cols] = acc

    def gates(r0, n):
        rows = slice(r0, r0 + n)
        za_scr[rows, :] = _silu(proj(r0, n, 2 * 512, 512))
        u_scr[rows, :] = _gelu(proj(r0, n, 3 * 512, 512))
        v_scr[rows, :] = _ln(_gelu(proj(r0, n, 4 * 512, 512)), sln_g_ref[...], sln_b_ref[...])
        zb_scr[rows, :] = _silu(proj(r0, n, 5 * 512, 512))

    lane = lax.broadcasted_iota(jnp.int32, (CHUNK, 2 * SGU_HEAD_DIM), 1)
    lo = lane < SGU_HEAD_DIM

    def spatial_gating(r0, n):
        for c in range(r0, r0 + n, CHUNK):
            for j in range(N_PAIRS):
                cols = slice(j * 2 * SGU_HEAD_DIM, (j + 1) * 2 * SGU_HEAD_DIM)
                vp = v_scr[c:c + CHUNK, cols]
                zero = jnp.zeros_like(vp)
                rhs = jnp.concatenate([jnp.where(lo, vp, zero), jnp.where(lo, zero, vp)], axis=0)
                sv = jnp.dot(wsm_scr[j], rhs, preferred_element_type=f32) + bsf_scr[:, cols]
                yb = u_scr[c:c + CHUNK, cols] * sv * zb_scr[c:c + CHUNK, cols]
                yb_scr[c:c + CHUNK, cols] = yb
        rows = slice(r0, r0 + n)
        h1_scr[rows, :] = ALPHA * h_scr[rows, :] + _mm(yb_scr[rows, :], w_out_scr[D_CONV:, :])

    def conv_out(r0, n):
        for r in range(r0, r0 + n, ROWS):
            rows = slice(r, r + ROWS)
            ya = _silu(_ln(conv_scr[rows, :], cln_g_ref[...], cln_b_ref[...])) * za_scr[rows, :]
            ya_scr[rows, :] = ya

    def mix_post(r0, n):
        rows = slice(r0, r0 + n)
        mix_a = _mm(ya_scr[rows, :], w_out_scr[0:D_CONV, :])
        h_scr[rows, :] = _ln(h1_scr[rows, :] + mix_a, pln_g_ref[...], pln_b_ref[...])

    def emit(r0, n):
        rows = slice(r0, r0 + n)
        pe = _mm(p_ref[0, 0, rows, :], w_ple_scr[...])
        gate = jax.nn.sigmoid(_mm(h_scr[rows, :], w_gate_scr[...]) + b_gate_ref[...])
        out_ref[0, rows, :] = h_scr[rows, :] + gate * pe

    tile = (0, TILE)
    input_ln(*tile)
    glu(0, *tile)
    conv_half(0, *tile)
    glu(1, *tile)
    gates(*tile)
    spatial_gating(*tile)
    conv_half(1, *tile)
    conv_out(*tile)
    a_scr[:, 0:HALO, :] = a_scr[:, TILE:TILE + HALO, :]
    mix_post(*tile)
    emit(*tile)


def kernel(x, p, ln_emb_g, ln_emb_b, w_in, conv_w, conv_b, conv_ln_g, conv_ln_b, sgu_ln_g, sgu_ln_b, w_s, b_s, w_out, post_ln_g, post_ln_b, w_ple, w_ple_gate, b_ple_gate):
    batch, seq, d_model = x.shape
    assert d_model == D_MODEL and seq % TILE == 0 and p.shape[0] == DEPTH
    f32 = jnp.float32
    bf16 = jnp.bfloat16

    def const(arr):
        return pl.BlockSpec(arr.shape, lambda b, s: (0,) * arr.ndim, pipeline_mode=pl.Buffered(1))

    params = (ln_emb_g, ln_emb_b, w_in, conv_w, conv_b, conv_ln_g, conv_ln_b, sgu_ln_g, sgu_ln_b,
              w_s, b_s, w_out, post_ln_g, post_ln_b, w_ple, w_ple_gate, b_ple_gate)
    in_specs = [
        pl.BlockSpec((1, TILE, D_MODEL), lambda b, s: (b, s, 0)),
        pl.BlockSpec((1, 1, TILE, D_PLE), lambda b, s: (0, b, s, 0)),
        *[const(arr) for arr in params],
    ]
    scratch = [
        pltpu.VMEM((TILE, D_MODEL), f32),
        pltpu.VMEM((N_CONV_SLABS, HALO + TILE, LANES), f32),
        pltpu.VMEM((TILE, D_CONV), f32),
        pltpu.VMEM((TILE, D_CONV), f32),
        pltpu.VMEM((TILE, D_SGU), f32),
        pltpu.VMEM((TILE, D_SGU), f32),
        pltpu.VMEM((TILE, D_SGU), f32),
        pltpu.VMEM((TILE, D_CONV), f32),
        pltpu.VMEM((TILE, D_SGU), f32),
        pltpu.VMEM((TILE, D_MODEL), f32),
        pltpu.VMEM((N_PAIRS, CHUNK, 2 * CHUNK), f32),
        pltpu.VMEM((CHUNK, D_SGU), f32),
        pltpu.VMEM(w_in.shape[1:], bf16),
        pltpu.VMEM(w_out.shape[1:], bf16),
        pltpu.VMEM(w_ple.shape[1:], bf16),
        pltpu.VMEM(w_ple_gate.shape[1:], bf16),
    ]
    return pl.pallas_call(
        _layer_kernel,
        grid=(batch, seq // TILE),
        in_specs=in_specs,
        out_specs=pl.BlockSpec((1, TILE, D_MODEL), lambda b, s: (b, s, 0)),
        out_shape=jax.ShapeDtypeStruct(x.shape, x.dtype),
        scratch_shapes=scratch,
        compiler_params=pltpu.CompilerParams(
            dimension_semantics=("arbitrary", "arbitrary"),
            vmem_limit_bytes=VMEM_LIMIT_BYTES),
        name="deepnorm_hybrid_layer",
    )(x, p, *params)
```

```python
import jax
import jax.numpy as jnp
from jax import lax
from jax.experimental import pallas as pl
from jax.experimental.pallas import tpu as pltpu

D_MODEL = 1024
D_PLE = 256
D_CONV = 512
D_SGU = 512
CONV_WIDTH = 31
N_SGU_HEADS = 8
SGU_HEAD_DIM = 64
CHUNK = 128
LN_EPS = 1e-5
DEPTH = 1
ALPHA = (2 * DEPTH) ** 0.25

LANES = 128
MXU_COLS = 256
TILE = 512
ROWS = 64
HALO = 32
N_PAIRS = N_SGU_HEADS // 2
N_CONV_SLABS = D_CONV // LANES
VMEM_LIMIT_BYTES = 60 * 1024 * 1024

_INV_SQRT2 = 0.7071067811865476


def _ln(x, g, b):
    mu = jnp.mean(x, axis=-1, keepdims=True)
    xc = x - mu
    var = jnp.mean(xc * xc, axis=-1, keepdims=True)
    return xc * lax.rsqrt(var + LN_EPS) * g + b


def _silu(x):
    return x * jax.nn.sigmoid(x)


def _gelu(x):
    return 0.5 * x * (1.0 + lax.erf(x * _INV_SQRT2))


def _mm(lhs_f32, w_bf16):
    return lax.dot_general(lhs_f32, w_bf16, (((1,), (0,)), ((), ())),
                           preferred_element_type=jnp.float32)


def _layer_kernel(x_ref, p_ref, ln_g_ref, ln_b_ref, w_in_ref, conv_w_ref, conv_b_ref,
                  cln_g_ref, cln_b_ref, sln_g_ref, sln_b_ref, ws_ref, bs_ref, w_out_ref,
                  pln_g_ref, pln_b_ref, w_ple_ref, w_gate_ref, b_gate_ref,
                  out_ref,
                  h_scr, a_scr, conv_scr, za_scr, u_scr, v_scr, zb_scr, ya_scr, yb_scr, h1_scr,
                  wsm_scr, bsf_scr, w_in_scr, w_out_scr, w_ple_scr, w_gate_scr):
    s = pl.program_id(1)
    f32 = jnp.float32

    @pl.when((pl.program_id(0) == 0) & (s == 0))
    def _():
        row = lax.broadcasted_iota(jnp.int32, (CHUNK, CHUNK), 0)
        col = lax.broadcasted_iota(jnp.int32, (CHUNK, CHUNK), 1)
        keep = col <= row
        for j in range(N_PAIRS):
            wsm_scr[j] = jnp.concatenate(
                [jnp.where(keep, ws_ref[0, 2 * j + n], 0.0) for n in range(2)], axis=1)
        bias_t = bs_ref[0].T
        for hd in range(N_SGU_HEADS):
            bsf_scr[:, hd * SGU_HEAD_DIM:(hd + 1) * SGU_HEAD_DIM] = jnp.broadcast_to(
                bias_t[:, hd:hd + 1], (CHUNK, SGU_HEAD_DIM))
        for dst, src in ((w_in_scr, w_in_ref), (w_out_scr, w_out_ref), (w_ple_scr, w_ple_ref),
                         (w_gate_scr, w_gate_ref)):
            for c in range(0, dst.shape[1], MXU_COLS):
                dst[:, c:c + MXU_COLS] = src[0, :, c:c + MXU_COLS].astype(jnp.bfloat16)

    @pl.when(s == 0)
    def _():
        a_scr[:, 0:HALO, :] = jnp.zeros((N_CONV_SLABS, HALO, LANES), f32)

    for r in range(0, TILE, ROWS):
        h_scr[r:r + ROWS, :] = _ln(x_ref[0, r:r + ROWS, :], ln_g_ref[...], ln_b_ref[...])

    def proj_cols(col0, width):
        return _mm(h_scr[...], w_in_scr[:, col0:col0 + width])

    def proj(g):
        return proj_cols(g * 512, 512)

    half_cols = D_CONV // 2
    slabs_per_half = N_CONV_SLABS // 2

    def conv_slab(c):
        cols = slice(c * LANES, (c + 1) * LANES)
        for r in range(0, TILE, ROWS):
            acc = jnp.broadcast_to(conv_b_ref[:, cols], (ROWS, LANES))
            for k in range(CONV_WIDTH):
                off = r + HALO - (CONV_WIDTH - 1) + k
                acc = acc + conv_w_ref[0, k:k + 1, cols] * a_scr[c, off:off + ROWS, :]
            conv_scr[r:r + ROWS, cols] = acc

    for n in range(2):
        val = proj_cols(n * half_cols, half_cols)
        gate = proj_cols(D_CONV + n * half_cols, half_cols)
        a = val * jax.nn.sigmoid(gate)
        for j in range(slabs_per_half):
            c = n * slabs_per_half + j
            a_scr[c, HALO:HALO + TILE, :] = a[:, j * LANES:(j + 1) * LANES]
        if n == 0:
            for j in range(slabs_per_half):
                conv_slab(j)
    za_scr[...] = _silu(proj(2))
    u_scr[...] = _gelu(proj(3))
    v_scr[...] = _ln(_gelu(proj(4)), sln_g_ref[...], sln_b_ref[...])
    zb_scr[...] = _silu(proj(5))

    lane = lax.broadcasted_iota(jnp.int32, (CHUNK, 2 * SGU_HEAD_DIM), 1)
    lo = lane < SGU_HEAD_DIM

    def block_diag(vp):
        zero = jnp.zeros_like(vp)
        return jnp.concatenate([jnp.where(lo, vp, zero), jnp.where(lo, zero, vp)], axis=0)

    for c0 in range(0, TILE, 2 * CHUNK):
        for j in range(N_PAIRS):
            cols = slice(j * 2 * SGU_HEAD_DIM, (j + 1) * 2 * SGU_HEAD_DIM)
            chunks = (c0, c0 + CHUNK)
            rhs = jnp.concatenate([block_diag(v_scr[c:c + CHUNK, cols]) for c in chunks], axis=1)
            sv2 = jnp.dot(wsm_scr[j], rhs, preferred_element_type=f32)
            for i, c in enumerate(chunks):
                sv = sv2[:, i * 2 * SGU_HEAD_DIM:(i + 1) * 2 * SGU_HEAD_DIM] + bsf_scr[:, cols]
                yb = u_scr[c:c + CHUNK, cols] * sv * zb_scr[c:c + CHUNK, cols]
                yb_scr[c:c + CHUNK, cols] = yb

    h1_scr[...] = ALPHA * h_scr[...] + _mm(yb_scr[...], w_out_scr[D_CONV:, :])

    for j in range(slabs_per_half):
        conv_slab(slabs_per_half + j)

    for r in range(0, TILE, ROWS):
        ya = (_silu(_ln(conv_scr[r:r + ROWS, :], cln_g_ref[...], cln_b_ref[...]))
              * za_scr[r:r + ROWS, :])
        ya_scr[r:r + ROWS, :] = ya
    a_scr[:, 0:HALO, :] = a_scr[:, TILE:TILE + HALO, :]

    mix_a = _mm(ya_scr[...], w_out_scr[0:D_CONV, :])
    h_scr[...] = _ln(h1_scr[...] + mix_a, pln_g_ref[...], pln_b_ref[...])

    pe = _mm(p_ref[0, 0], w_ple_scr[...])
    gate = jax.nn.sigmoid(_mm(h_scr[...], w_gate_scr[...]) + b_gate_ref[...])
    out_ref[0] = h_scr[...] + gate * pe


def kernel(x, p, ln_emb_g, ln_emb_b, w_in, conv_w, conv_b, conv_ln_g, conv_ln_b, sgu_ln_g, sgu_ln_b, w_s, b_s, w_out, post_ln_g, post_ln_b, w_ple, w_ple_gate, b_ple_gate):
    batch, seq, d_model = x.shape
    assert d_model == D_MODEL and seq % TILE == 0 and p.shape[0] == DEPTH
    f32 = jnp.float32
    bf16 = jnp.bfloat16

    def const(arr):
        return pl.BlockSpec(arr.shape, lambda b, s: (0,) * arr.ndim, pipeline_mode=pl.Buffered(1))

    params = (ln_emb_g, ln_emb_b, w_in, conv_w, conv_b, conv_ln_g, conv_ln_b, sgu_ln_g, sgu_ln_b,
              w_s, b_s, w_out, post_ln_g, post_ln_b, w_ple, w_ple_gate, b_ple_gate)
    in_specs = [
        pl.BlockSpec((1, TILE, D_MODEL), lambda b, s: (b, s, 0)),
        pl.BlockSpec((1, 1, TILE, D_PLE), lambda b, s: (0, b, s, 0)),
        *[const(arr) for arr in params],
    ]
    scratch = [
        pltpu.VMEM((TILE, D_MODEL), f32),
        pltpu.VMEM((N_CONV_SLABS, HALO + TILE, LANES), f32),
        pltpu.VMEM((TILE, D_CONV), f32),
        pltpu.VMEM((TILE, D_CONV), f32),
        pltpu.VMEM((TILE, D_SGU), f32),
        pltpu.VMEM((TILE, D_SGU), f32),
        pltpu.VMEM((TILE, D_SGU), f32),
        pltpu.VMEM((TILE, D_CONV), f32),
        pltpu.VMEM((TILE, D_SGU), f32),
        pltpu.VMEM((TILE, D_MODEL), f32),
        pltpu.VMEM((N_PAIRS, CHUNK, 2 * CHUNK), f32),
        pltpu.VMEM((CHUNK, D_SGU), f32),
        pltpu.VMEM(w_in.shape[1:], bf16),
        pltpu.VMEM(w_out.shape[1:], bf16),
        pltpu.VMEM(w_ple.shape[1:], bf16),
        pltpu.VMEM(w_ple_gate.shape[1:], bf16),
    ]
    return pl.pallas_call(
        _layer_kernel,
        grid=(batch, seq // TILE),
        in_specs=in_specs,
        out_specs=pl.BlockSpec((1, TILE, D_MODEL), lambda b, s: (b, s, 0)),
        out_shape=jax.ShapeDtypeStruct(x.shape, x.dtype),
        scratch_shapes=scratch,
        compiler_params=pltpu.CompilerParams(
            dimension_semantics=("arbitrary", "arbitrary"),
            vmem_limit_bytes=VMEM_LIMIT_BYTES),
        name="deepnorm_hybrid_layer",
    )(x, p, *params)
```

```python
import jax
import jax.numpy as jnp
from jax import lax
from jax.experimental import pallas as pl
from jax.experimental.pallas import tpu as pltpu

D_MODEL = 1024
D_PLE = 256
D_CONV = 512
D_SGU = 512
CONV_WIDTH = 31
N_SGU_HEADS = 8
SGU_HEAD_DIM = 64
CHUNK = 128
LN_EPS = 1e-5
DEPTH = 1
ALPHA = (2 * DEPTH) ** 0.25

LANES = 128
MXU_COLS = 256
TILE = 512
ROWS = 64
CONV_ROWS = 32
HALO = 32
N_PAIRS = N_SGU_HEADS // 2
N_CONV_SLABS = D_CONV // LANES
VMEM_LIMIT_BYTES = 60 * 1024 * 1024

_INV_SQRT2 = 0.7071067811865476


def _ln(x, g, b):
    mu = jnp.mean(x, axis=-1, keepdims=True)
    xc = x - mu
    var = jnp.mean(xc * xc, axis=-1, keepdims=True)
    return xc * lax.rsqrt(var + LN_EPS) * g + b


def _silu(x):
    return x * jax.nn.sigmoid(x)


def _gelu(x):
    return 0.5 * x * (1.0 + lax.erf(x * _INV_SQRT2))


def _mm(lhs_f32, w_bf16):
    return lax.dot_general(lhs_f32, w_bf16, (((1,), (0,)), ((), ())),
                           preferred_element_type=jnp.float32)


def _layer_kernel(x_ref, p_ref, ln_g_ref, ln_b_ref, w_in_ref, conv_w_ref, conv_b_ref,
                  cln_g_ref, cln_b_ref, sln_g_ref, sln_b_ref, ws_ref, bs_ref, w_out_ref,
                  pln_g_ref, pln_b_ref, w_ple_ref, w_gate_ref, b_gate_ref,
                  out_ref,
                  h_scr, a_scr, conv_scr, za_scr, u_scr, v_scr, zb_scr, ya_scr, yb_scr, h1_scr,
                  wsm_scr, bsf_scr, w_in_scr, w_out_scr, w_ple_scr, w_gate_scr):
    s = pl.program_id(1)
    f32 = jnp.float32

    @pl.when((pl.program_id(0) == 0) & (s == 0))
    def _():
        row = lax.broadcasted_iota(jnp.int32, (CHUNK, CHUNK), 0)
        col = lax.broadcasted_iota(jnp.int32, (CHUNK, CHUNK), 1)
        keep = col <= row
        for j in range(N_PAIRS):
            wsm_scr[j] = jnp.concatenate(
                [jnp.where(keep, ws_ref[0, 2 * j + n], 0.0) for n in range(2)], axis=1)
        bias_t = bs_ref[0].T
        for hd in range(N_SGU_HEADS):
            bsf_scr[:, hd * SGU_HEAD_DIM:(hd + 1) * SGU_HEAD_DIM] = jnp.broadcast_to(
                bias_t[:, hd:hd + 1], (CHUNK, SGU_HEAD_DIM))
        for dst, src in ((w_in_scr, w_in_ref), (w_out_scr, w_out_ref), (w_ple_scr, w_ple_ref),
                         (w_gate_scr, w_gate_ref)):
            for c in range(0, dst.shape[1], MXU_COLS):
                dst[:, c:c + MXU_COLS] = src[0, :, c:c + MXU_COLS].astype(jnp.bfloat16)

    @pl.when(s == 0)
    def _():
        a_scr[:, 0:HALO, :] = jnp.zeros((N_CONV_SLABS, HALO, LANES), f32)

    for r in range(0, TILE, ROWS):
        h_scr[r:r + ROWS, :] = _ln(x_ref[0, r:r + ROWS, :], ln_g_ref[...], ln_b_ref[...])

    def proj_cols(col0, width):
        return _mm(h_scr[...], w_in_scr[:, col0:col0 + width])

    def proj(g):
        return proj_cols(g * 512, 512)

    half_cols = D_CONV // 2
    slabs_per_half = N_CONV_SLABS // 2

    def conv_slab(c):
        cols = slice(c * LANES, (c + 1) * LANES)
        for r in range(0, TILE, CONV_ROWS):
            acc = jnp.broadcast_to(conv_b_ref[:, cols], (CONV_ROWS, LANES))
            for k in range(CONV_WIDTH):
                off = r + HALO - (CONV_WIDTH - 1) + k
                acc = acc + conv_w_ref[0, k:k + 1, cols] * a_scr[c, off:off + CONV_ROWS, :]
            conv_scr[r:r + CONV_ROWS, cols] = acc

    for n in range(2):
        val = proj_cols(n * half_cols, half_cols)
        gate = proj_cols(D_CONV + n * half_cols, half_cols)
        a = val * jax.nn.sigmoid(gate)
        for j in range(slabs_per_half):
            c = n * slabs_per_half + j
            a_scr[c, HALO:HALO + TILE, :] = a[:, j * LANES:(j + 1) * LANES]
        if n == 0:
            for j in range(slabs_per_half):
                conv_slab(j)
    za_scr[...] = _silu(proj(2))
    u_scr[...] = _gelu(proj(3))
    v_scr[...] = _ln(_gelu(proj(4)), sln_g_ref[...], sln_b_ref[...])
    zb_scr[...] = _silu(proj(5))

    lane = lax.broadcasted_iota(jnp.int32, (CHUNK, 2 * SGU_HEAD_DIM), 1)
    lo = lane < SGU_HEAD_DIM
    def block_diag(vp):
        zero = jnp.zeros_like(vp)
        return jnp.concatenate([jnp.where(lo, vp, zero), jnp.where(lo, zero, vp)], axis=0)

    for j in range(N_PAIRS):
        for c0 in range(0, TILE, 2 * CHUNK):
            cols = slice(j * 2 * SGU_HEAD_DIM, (j + 1) * 2 * SGU_HEAD_DIM)
            chunks = (c0, c0 + CHUNK)
            rhs = jnp.concatenate([block_diag(v_scr[c:c + CHUNK, cols]) for c in chunks], axis=1)
            sv2 = jnp.dot(wsm_scr[j], rhs, preferred_element_type=f32)
            for i, c in enumerate(chunks):
                sv = sv2[:, i * 2 * SGU_HEAD_DIM:(i + 1) * 2 * SGU_HEAD_DIM] + bsf_scr[:, cols]
                yb = u_scr[c:c + CHUNK, cols] * sv * zb_scr[c:c + CHUNK, cols]
                yb_scr[c:c + CHUNK, cols] = yb

    h1_scr[...] = ALPHA * h_scr[...] + _mm(yb_scr[...], w_out_scr[D_CONV:, :])

    for j in range(slabs_per_half):
        conv_slab(slabs_per_half + j)

    for r in range(0, TILE, ROWS):
        ya = (_silu(_ln(conv_scr[r:r + ROWS, :], cln_g_ref[...], cln_b_ref[...]))
              * za_scr[r:r + ROWS, :])
        ya_scr[r:r + ROWS, :] = ya
    a_scr[:, 0:HALO, :] = a_scr[:, TILE:TILE + HALO, :]

    mix_a = _mm(ya_scr[...], w_out_scr[0:D_CONV, :])
    h_scr[...] = _ln(h1_scr[...] + mix_a, pln_g_ref[...], pln_b_ref[...])

    pe = _mm(p_ref[0, 0], w_ple_scr[...])
    gate = jax.nn.sigmoid(_mm(h_scr[...], w_gate_scr[...]) + b_gate_ref[...])
    out_ref[0] = h_scr[...] + gate * pe


def kernel(x, p, ln_emb_g, ln_emb_b, w_in, conv_w, conv_b, conv_ln_g, conv_ln_b, sgu_ln_g, sgu_ln_b, w_s, b_s, w_out, post_ln_g, post_ln_b, w_ple, w_ple_gate, b_ple_gate):
    batch, seq, d_model = x.shape
    assert d_model == D_MODEL and seq % TILE == 0 and p.shape[0] == DEPTH
    f32 = jnp.float32
    bf16 = jnp.bfloat16

    def const(arr):
        return pl.BlockSpec(arr.shape, lambda b, s: (0,) * arr.ndim, pipeline_mode=pl.Buffered(1))

    params = (ln_emb_g, ln_emb_b, w_in, conv_w, conv_b, conv_ln_g, conv_ln_b, sgu_ln_g, sgu_ln_b,
              w_s, b_s, w_out, post_ln_g, post_ln_b, w_ple, w_ple_gate, b_ple_gate)
    in_specs = [
        pl.BlockSpec((1, TILE, D_MODEL), lambda b, s: (b, s, 0)),
        pl.BlockSpec((1, 1, TILE, D_PLE), lambda b, s: (0, b, s, 0)),
        *[const(arr) for arr in params],
    ]
    scratch = [
        pltpu.VMEM((TILE, D_MODEL), f32),
        pltpu.VMEM((N_CONV_SLABS, HALO + TILE, LANES), f32),
        pltpu.VMEM((TILE, D_CONV), f32),
        pltpu.VMEM((TILE, D_CONV), f32),
        pltpu.VMEM((TILE, D_SGU), f32),
        pltpu.VMEM((TILE, D_SGU), f32),
        pltpu.VMEM((TILE, D_SGU), f32),
        pltpu.VMEM((TILE, D_CONV), f32),
        pltpu.VMEM((TILE, D_SGU), f32),
        pltpu.VMEM((TILE, D_MODEL), f32),
        pltpu.VMEM((N_PAIRS, CHUNK, 2 * CHUNK), f32),
        pltpu.VMEM((CHUNK, D_SGU), f32),
        pltpu.VMEM(w_in.shape[1:], bf16),
        pltpu.VMEM(w_out.shape[1:], bf16),
        pltpu.VMEM(w_ple.shape[1:], bf16),
        pltpu.VMEM(w_ple_gate.shape[1:], bf16),
    ]
    return pl.pallas_call(
        _layer_kernel,
        grid=(batch, seq // TILE),
        in_specs=in_specs,
        out_specs=pl.BlockSpec((1, TILE, D_MODEL), lambda b, s: (b, s, 0)),
        out_shape=jax.ShapeDtypeStruct(x.shape, x.dtype),
        scratch_shapes=scratch,
        compiler_params=pltpu.CompilerParams(
            dimension_semantics=("arbitrary", "arbitrary"),
            vmem_limit_bytes=VMEM_LIMIT_BYTES),
        name="deepnorm_hybrid_layer",
    )(x, p, *params)
```

```python
import jax
import jax.numpy as jnp
from jax import lax
from jax.experimental import pallas as pl
from jax.experimental.pallas import tpu as pltpu

D_MODEL = 1024
D_PLE = 256
D_CONV = 512
D_SGU = 512
CONV_WIDTH = 31
N_SGU_HEADS = 8
SGU_HEAD_DIM = 64
CHUNK = 128
LN_EPS = 1e-5
DEPTH = 1
ALPHA = (2 * DEPTH) ** 0.25

LANES = 128
MXU_COLS = 256
TILE = 512
ROWS = 64
CONV_ROWS = 128
HALO = 32
N_PAIRS = N_SGU_HEADS // 2
N_CONV_SLABS = D_CONV // LANES
VMEM_LIMIT_BYTES = 60 * 1024 * 1024

_INV_SQRT2 = 0.7071067811865476


def _ln(x, g, b):
    mu = jnp.mean(x, axis=-1, keepdims=True)
    xc = x - mu
    var = jnp.mean(xc * xc, axis=-1, keepdims=True)
    return xc * lax.rsqrt(var + LN_EPS) * g + b


def _silu(x):
    return x * jax.nn.sigmoid(x)


def _gelu(x):
    return 0.5 * x * (1.0 + lax.erf(x * _INV_SQRT2))


def _mm(lhs_f32, w_bf16):
    return lax.dot_general(lhs_f32, w_bf16, (((1,), (0,)), ((), ())),
                           preferred_element_type=jnp.float32)


def _layer_kernel(x_ref, p_ref, ln_g_ref, ln_b_ref, w_in_ref, conv_w_ref, conv_b_ref,
                  cln_g_ref, cln_b_ref, sln_g_ref, sln_b_ref, ws_ref, bs_ref, w_out_ref,
                  pln_g_ref, pln_b_ref, w_ple_ref, w_gate_ref, b_gate_ref,
                  out_ref,
                  h_scr, a_scr, conv_scr, za_scr, u_scr, v_scr, zb_scr, ya_scr, yb_scr, h1_scr,
                  wsm_scr, bsf_scr, w_in_scr, w_out_scr, w_ple_scr, w_gate_scr):
    s = pl.program_id(1)
    f32 = jnp.float32

    @pl.when((pl.program_id(0) == 0) & (s == 0))
    def _():
        row = lax.broadcasted_iota(jnp.int32, (CHUNK, CHUNK), 0)
        col = lax.broadcasted_iota(jnp.int32, (CHUNK, CHUNK), 1)
        keep = col <= row
        for j in range(N_PAIRS):
            wsm_scr[j] = jnp.concatenate(
                [jnp.where(keep, ws_ref[0, 2 * j + n], 0.0) for n in range(2)], axis=1)
        bias_t = bs_ref[0].T
        for hd in range(N_SGU_HEADS):
            bsf_scr[:, hd * SGU_HEAD_DIM:(hd + 1) * SGU_HEAD_DIM] = jnp.broadcast_to(
                bias_t[:, hd:hd + 1], (CHUNK, SGU_HEAD_DIM))
        for dst, src in ((w_in_scr, w_in_ref), (w_out_scr, w_out_ref), (w_ple_scr, w_ple_ref),
                         (w_gate_scr, w_gate_ref)):
            for c in range(0, dst.shape[1], MXU_COLS):
                dst[:, c:c + MXU_COLS] = src[0, :, c:c + MXU_COLS].astype(jnp.bfloat16)

    @pl.when(s == 0)
    def _():
        a_scr[:, 0:HALO, :] = jnp.zeros((N_CONV_SLABS, HALO, LANES), f32)

    for r in range(0, TILE, ROWS):
        h_scr[r:r + ROWS, :] = _ln(x_ref[0, r:r + ROWS, :], ln_g_ref[...], ln_b_ref[...])

    def proj_cols(col0, width):
        return _mm(h_scr[...], w_in_scr[:, col0:col0 + width])

    def proj(g):
        return proj_cols(g * 512, 512)

    half_cols = D_CONV // 2
    slabs_per_half = N_CONV_SLABS // 2

    def conv_slab(c):
        cols = slice(c * LANES, (c + 1) * LANES)
        for r in range(0, TILE, CONV_ROWS):
            acc = jnp.broadcast_to(conv_b_ref[:, cols], (CONV_ROWS, LANES))
            for k in range(CONV_WIDTH):
                off = r + HALO - (CONV_WIDTH - 1) + k
                acc = acc + conv_w_ref[0, k:k + 1, cols] * a_scr[c, off:off + CONV_ROWS, :]
            conv_scr[r:r + CONV_ROWS, cols] = acc

    for n in range(2):
        val = proj_cols(n * half_cols, half_cols)
        gate = proj_cols(D_CONV + n * half_cols, half_cols)
        a = val * jax.nn.sigmoid(gate)
        for j in range(slabs_per_half):
            c = n * slabs_per_half + j
            a_scr[c, HALO:HALO + TILE, :] = a[:, j * LANES:(j + 1) * LANES]
        if n == 0:
            for j in range(slabs_per_half):
                conv_slab(j)
    za_scr[...] = _silu(proj(2))
    u_scr[...] = _gelu(proj(3))
    v_scr[...] = _ln(_gelu(proj(4)), sln_g_ref[...], sln_b_ref[...])
    zb_scr[...] = _silu(proj(5))

    lane = lax.broadcasted_iota(jnp.int32, (CHUNK, 2 * SGU_HEAD_DIM), 1)
    lo = lane < SGU_HEAD_DIM
    def block_diag(vp):
        zero = jnp.zeros_like(vp)
        return jnp.concatenate([jnp.where(lo, vp, zero), jnp.where(lo, zero, vp)], axis=0)

    for j in range(N_PAIRS):
        for c0 in range(0, TILE, 2 * CHUNK):
            cols = slice(j * 2 * SGU_HEAD_DIM, (j + 1) * 2 * SGU_HEAD_DIM)
            chunks = (c0, c0 + CHUNK)
            rhs = jnp.concatenate([block_diag(v_scr[c:c + CHUNK, cols]) for c in chunks], axis=1)
            sv2 = jnp.dot(wsm_scr[j], rhs, preferred_element_type=f32)
            for i, c in enumerate(chunks):
                sv = sv2[:, i * 2 * SGU_HEAD_DIM:(i + 1) * 2 * SGU_HEAD_DIM] + bsf_scr[:, cols]
                yb = u_scr[c:c + CHUNK, cols] * sv * zb_scr[c:c + CHUNK, cols]
                yb_scr[c:c + CHUNK, cols] = yb

    h1_scr[...] = ALPHA * h_scr[...] + _mm(yb_scr[...], w_out_scr[D_CONV:, :])

    for j in range(slabs_per_half):
        conv_slab(slabs_per_half + j)

    for r in range(0, TILE, ROWS):
        ya = (_silu(_ln(conv_scr[r:r + ROWS, :], cln_g_ref[...], cln_b_ref[...]))
              * za_scr[r:r + ROWS, :])
        ya_scr[r:r + ROWS, :] = ya
    a_scr[:, 0:HALO, :] = a_scr[:, TILE:TILE + HALO, :]

    mix_a = _mm(ya_scr[...], w_out_scr[0:D_CONV, :])
    h_scr[...] = _ln(h1_scr[...] + mix_a, pln_g_ref[...], pln_b_ref[...])

    pe = _mm(p_ref[0, 0], w_ple_scr[...])
    gate = jax.nn.sigmoid(_mm(h_scr[...], w_gate_scr[...]) + b_gate_ref[...])
    out_ref[0] = h_scr[...] + gate * pe


def kernel(x, p, ln_emb_g, ln_emb_b, w_in, conv_w, conv_b, conv_ln_g, conv_ln_b, sgu_ln_g, sgu_ln_b, w_s, b_s, w_out, post_ln_g, post_ln_b, w_ple, w_ple_gate, b_ple_gate):
    batch, seq, d_model = x.shape
    assert d_model == D_MODEL and seq % TILE == 0 and p.shape[0] == DEPTH
    f32 = jnp.float32
    bf16 = jnp.bfloat16

    def const(arr):
        return pl.BlockSpec(arr.shape, lambda b, s: (0,) * arr.ndim, pipeline_mode=pl.Buffered(1))

    params = (ln_emb_g, ln_emb_b, w_in, conv_w, conv_b, conv_ln_g, conv_ln_b, sgu_ln_g, sgu_ln_b,
              w_s, b_s, w_out, post_ln_g, post_ln_b, w_ple, w_ple_gate, b_ple_gate)
    in_specs = [
        pl.BlockSpec((1, TILE, D_MODEL), lambda b, s: (b, s, 0)),
        pl.BlockSpec((1, 1, TILE, D_PLE), lambda b, s: (0, b, s, 0)),
        *[const(arr) for arr in params],
    ]
    scratch = [
        pltpu.VMEM((TILE, D_MODEL), f32),
        pltpu.VMEM((N_CONV_SLABS, HALO + TILE, LANES), f32),
        pltpu.VMEM((TILE, D_CONV), f32),
        pltpu.VMEM((TILE, D_CONV), f32),
        pltpu.VMEM((TILE, D_SGU), f32),
        pltpu.VMEM((TILE, D_SGU), f32),
        pltpu.VMEM((TILE, D_SGU), f32),
        pltpu.VMEM((TILE, D_CONV), f32),
        pltpu.VMEM((TILE, D_SGU), f32),
        pltpu.VMEM((TILE, D_MODEL), f32),
        pltpu.VMEM((N_PAIRS, CHUNK, 2 * CHUNK), f32),
        pltpu.VMEM((CHUNK, D_SGU), f32),
        pltpu.VMEM(w_in.shape[1:], bf16),
        pltpu.VMEM(w_out.shape[1:], bf16),
        pltpu.VMEM(w_ple.shape[1:], bf16),
        pltpu.VMEM(w_ple_gate.shape[1:], bf16),
    ]
    return pl.pallas_call(
        _layer_kernel,
        grid=(batch, seq // TILE),
        in_specs=in_specs,
        out_specs=pl.BlockSpec((1, TILE, D_MODEL), lambda b, s: (b, s, 0)),
        out_shape=jax.ShapeDtypeStruct(x.shape, x.dtype),
        scratch_shapes=scratch,
        compiler_params=pltpu.CompilerParams(
            dimension_semantics=("arbitrary", "arbitrary"),
            vmem_limit_bytes=VMEM_LIMIT_BYTES),
        name="deepnorm_hybrid_layer",
    )(x, p, *params)
```

```python
import jax
import jax.numpy as jnp
from jax import lax
from jax.experimental import pallas as pl
from jax.experimental.pallas import tpu as pltpu

D_MODEL = 1024
D_PLE = 256
D_CONV = 512
D_SGU = 512
CONV_WIDTH = 31
N_SGU_HEADS = 8
SGU_HEAD_DIM = 64
CHUNK = 128
LN_EPS = 1e-5
DEPTH = 1
ALPHA = (2 * DEPTH) ** 0.25

LANES = 128
MXU_COLS = 256
TILE = 512
ROWS = 64
CONV_ROWS = 128
SGU_CHUNKS = 4
HALO = 32
N_PAIRS = N_SGU_HEADS // 2
N_CONV_SLABS = D_CONV // LANES
VMEM_LIMIT_BYTES = 60 * 1024 * 1024

_INV_SQRT2 = 0.7071067811865476


def _ln(x, g, b):
    mu = jnp.mean(x, axis=-1, keepdims=True)
    xc = x - mu
    var = jnp.mean(xc * xc, axis=-1, keepdims=True)
    return xc * lax.rsqrt(var + LN_EPS) * g + b


def _silu(x):
    return x * jax.nn.sigmoid(x)


def _gelu(x):
    return 0.5 * x * (1.0 + lax.erf(x * _INV_SQRT2))


def _mm(lhs_f32, w_bf16):
    return lax.dot_general(lhs_f32, w_bf16, (((1,), (0,)), ((), ())),
                           preferred_element_type=jnp.float32)


def _layer_kernel(x_ref, p_ref, ln_g_ref, ln_b_ref, w_in_ref, conv_w_ref, conv_b_ref,
                  cln_g_ref, cln_b_ref, sln_g_ref, sln_b_ref, ws_ref, bs_ref, w_out_ref,
                  pln_g_ref, pln_b_ref, w_ple_ref, w_gate_ref, b_gate_ref,
                  out_ref,
                  h_scr, a_scr, conv_scr, za_scr, u_scr, v_scr, zb_scr, ya_scr, yb_scr, h1_scr,
                  wsm_scr, bsf_scr, w_in_scr, w_out_scr, w_ple_scr, w_gate_scr):
    s = pl.program_id(1)
    f32 = jnp.float32

    @pl.when((pl.program_id(0) == 0) & (s == 0))
    def _():
        row = lax.broadcasted_iota(jnp.int32, (CHUNK, CHUNK), 0)
        col = lax.broadcasted_iota(jnp.int32, (CHUNK, CHUNK), 1)
        keep = col <= row
        for j in range(N_PAIRS):
            wsm_scr[j] = jnp.concatenate(
                [jnp.where(keep, ws_ref[0, 2 * j + n], 0.0) for n in range(2)], axis=1)
        bias_t = bs_ref[0].T
        for hd in range(N_SGU_HEADS):
            bsf_scr[:, hd * SGU_HEAD_DIM:(hd + 1) * SGU_HEAD_DIM] = jnp.broadcast_to(
                bias_t[:, hd:hd + 1], (CHUNK, SGU_HEAD_DIM))
        for dst, src in ((w_in_scr, w_in_ref), (w_out_scr, w_out_ref), (w_ple_scr, w_ple_ref),
                         (w_gate_scr, w_gate_ref)):
            for c in range(0, dst.shape[1], MXU_COLS):
                dst[:, c:c + MXU_COLS] = src[0, :, c:c + MXU_COLS].astype(jnp.bfloat16)

    @pl.when(s == 0)
    def _():
        a_scr[:, 0:HALO, :] = jnp.zeros((N_CONV_SLABS, HALO, LANES), f32)

    for r in range(0, TILE, ROWS):
        h_scr[r:r + ROWS, :] = _ln(x_ref[0, r:r + ROWS, :], ln_g_ref[...], ln_b_ref[...])

    def proj_cols(col0, width):
        return _mm(h_scr[...], w_in_scr[:, col0:col0 + width])

    def proj(g):
        return proj_cols(g * 512, 512)

    half_cols = D_CONV // 2
    slabs_per_half = N_CONV_SLABS // 2

    def conv_slab(c):
        cols = slice(c * LANES, (c + 1) * LANES)
        for r in range(0, TILE, CONV_ROWS):
            acc = jnp.broadcast_to(conv_b_ref[:, cols], (CONV_ROWS, LANES))
            for k in range(CONV_WIDTH):
                off = r + HALO - (CONV_WIDTH - 1) + k
                acc = acc + conv_w_ref[0, k:k + 1, cols] * a_scr[c, off:off + CONV_ROWS, :]
            conv_scr[r:r + CONV_ROWS, cols] = acc

    for n in range(2):
        val = proj_cols(n * half_cols, half_cols)
        gate = proj_cols(D_CONV + n * half_cols, half_cols)
        a = val * jax.nn.sigmoid(gate)
        for j in range(slabs_per_half):
            c = n * slabs_per_half + j
            a_scr[c, HALO:HALO + TILE, :] = a[:, j * LANES:(j + 1) * LANES]
        if n == 0:
            for j in range(slabs_per_half):
                conv_slab(j)
    za_scr[...] = _silu(proj(2))
    u_scr[...] = _gelu(proj(3))
    v_scr[...] = _ln(_gelu(proj(4)), sln_g_ref[...], sln_b_ref[...])
    zb_scr[...] = _silu(proj(5))

    lane = lax.broadcasted_iota(jnp.int32, (CHUNK, 2 * SGU_HEAD_DIM), 1)
    lo = lane < SGU_HEAD_DIM
    def block_diag(vp):
        zero = jnp.zeros_like(vp)
        return jnp.concatenate([jnp.where(lo, vp, zero), jnp.where(lo, zero, vp)], axis=0)

    for c0 in range(0, TILE, SGU_CHUNKS * CHUNK):
        for j in range(N_PAIRS):
            cols = slice(j * 2 * SGU_HEAD_DIM, (j + 1) * 2 * SGU_HEAD_DIM)
            chunks = tuple(range(c0, c0 + SGU_CHUNKS * CHUNK, CHUNK))
            rhs = jnp.concatenate([block_diag(v_scr[c:c + CHUNK, cols]) for c in chunks], axis=1)
            sv2 = jnp.dot(wsm_scr[j], rhs, preferred_element_type=f32)
            for i, c in enumerate(chunks):
                sv = sv2[:, i * 2 * SGU_HEAD_DIM:(i + 1) * 2 * SGU_HEAD_DIM] + bsf_scr[:, cols]
                yb = u_scr[c:c + CHUNK, cols] * sv * zb_scr[c:c + CHUNK, cols]
                yb_scr[c:c + CHUNK, cols] = yb

    h1_scr[...] = ALPHA * h_scr[...] + _mm(yb_scr[...], w_out_scr[D_CONV:, :])

    for j in range(slabs_per_half):
        conv_slab(slabs_per_half + j)

    for r in range(0, TILE, ROWS):
        ya = (_silu(_ln(conv_scr[r:r + ROWS, :], cln_g_ref[...], cln_b_ref[...]))
              * za_scr[r:r + ROWS, :])
        ya_scr[r:r + ROWS, :] = ya
    a_scr[:, 0:HALO, :] = a_scr[:, TILE:TILE + HALO, :]

    mix_a = _mm(ya_scr[...], w_out_scr[0:D_CONV, :])
    h_scr[...] = _ln(h1_scr[...] + mix_a, pln_g_ref[...], pln_b_ref[...])

    pe = _mm(p_ref[0, 0], w_ple_scr[...])
    gate = jax.nn.sigmoid(_mm(h_scr[...], w_gate_scr[...]) + b_gate_ref[...])
    out_ref[0] = h_scr[...] + gate * pe


def kernel(x, p, ln_emb_g, ln_emb_b, w_in, conv_w, conv_b, conv_ln_g, conv_ln_b, sgu_ln_g, sgu_ln_b, w_s, b_s, w_out, post_ln_g, post_ln_b, w_ple, w_ple_gate, b_ple_gate):
    batch, seq, d_model = x.shape
    assert d_model == D_MODEL and seq % TILE == 0 and p.shape[0] == DEPTH
    f32 = jnp.float32
    bf16 = jnp.bfloat16

    def const(arr):
        return pl.BlockSpec(arr.shape, lambda b, s: (0,) * arr.ndim, pipeline_mode=pl.Buffered(1))

    params = (ln_emb_g, ln_emb_b, w_in, conv_w, conv_b, conv_ln_g, conv_ln_b, sgu_ln_g, sgu_ln_b,
              w_s, b_s, w_out, post_ln_g, post_ln_b, w_ple, w_ple_gate, b_ple_gate)
    in_specs = [
        pl.BlockSpec((1, TILE, D_MODEL), lambda b, s: (b, s, 0)),
        pl.BlockSpec((1, 1, TILE, D_PLE), lambda b, s: (0, b, s, 0)),
        *[const(arr) for arr in params],
    ]
    scratch = [
        pltpu.VMEM((TILE, D_MODEL), f32),
        pltpu.VMEM((N_CONV_SLABS, HALO + TILE, LANES), f32),
        pltpu.VMEM((TILE, D_CONV), f32),
        pltpu.VMEM((TILE, D_CONV), f32),
        pltpu.VMEM((TILE, D_SGU), f32),
        pltpu.VMEM((TILE, D_SGU), f32),
        pltpu.VMEM((TILE, D_SGU), f32),
        pltpu.VMEM((TILE, D_CONV), f32),
        pltpu.VMEM((TILE, D_SGU), f32),
        pltpu.VMEM((TILE, D_MODEL), f32),
        pltpu.VMEM((N_PAIRS, CHUNK, 2 * CHUNK), f32),
        pltpu.VMEM((CHUNK, D_SGU), f32),
        pltpu.VMEM(w_in.shape[1:], bf16),
        pltpu.VMEM(w_out.shape[1:], bf16),
        pltpu.VMEM(w_ple.shape[1:], bf16),
        pltpu.VMEM(w_ple_gate.shape[1:], bf16),
    ]
    return pl.pallas_call(
        _layer_kernel,
        grid=(batch, seq // TILE),
        in_specs=in_specs,
        out_specs=pl.BlockSpec((1, TILE, D_MODEL), lambda b, s: (b, s, 0)),
        out_shape=jax.ShapeDtypeStruct(x.shape, x.dtype),
        scratch_shapes=scratch,
        compiler_params=pltpu.CompilerParams(
            dimension_semantics=("arbitrary", "arbitrary"),
            vmem_limit_bytes=VMEM_LIMIT_BYTES),
        name="deepnorm_hybrid_layer",
    )(x, p, *params)
```

```python
import jax
import jax.numpy as jnp
from jax import lax
from jax.experimental import pallas as pl
from jax.experimental.pallas import tpu as pltpu

D_MODEL = 1024
D_PLE = 256
D_CONV = 512
D_SGU = 512
CONV_WIDTH = 31
N_SGU_HEADS = 8
SGU_HEAD_DIM = 64
CHUNK = 128
LN_EPS = 1e-5
DEPTH = 1
ALPHA = (2 * DEPTH) ** 0.25

LANES = 128
MXU_COLS = 256
TILE = 512
ROWS = 64
CONV_ROWS = 128
HALO = 32
N_PAIRS = N_SGU_HEADS // 2
N_CONV_SLABS = D_CONV // LANES
VMEM_LIMIT_BYTES = 60 * 1024 * 1024

_INV_SQRT2 = 0.7071067811865476


def _ln(x, g, b):
    mu = jnp.mean(x, axis=-1, keepdims=True)
    xc = x - mu
    var = jnp.mean(xc * xc, axis=-1, keepdims=True)
    return xc * lax.rsqrt(var + LN_EPS) * g + b


def _silu(x):
    return x * jax.nn.sigmoid(x)


def _gelu(x):
    return 0.5 * x * (1.0 + lax.erf(x * _INV_SQRT2))


def _mm(lhs_f32, w_bf16):
    return lax.dot_general(lhs_f32, w_bf16, (((1,), (0,)), ((), ())),
                           preferred_element_type=jnp.float32)


def _layer_kernel(x_ref, p_ref, ln_g_ref, ln_b_ref, w_in_ref, conv_w_ref, conv_b_ref,
                  cln_g_ref, cln_b_ref, sln_g_ref, sln_b_ref, ws_ref, bs_ref, w_out_ref,
                  pln_g_ref, pln_b_ref, w_ple_ref, w_gate_ref, b_gate_ref,
                  out_ref,
                  h_scr, a_scr, conv_scr, za_scr, u_scr, v_scr, zb_scr, ya_scr, yb_scr, h1_scr,
                  wsm_scr, bsf_scr, w_in_scr, w_out_scr, w_ple_scr, w_gate_scr, dma_sem):
    s = pl.program_id(1)
    f32 = jnp.float32

    @pl.when((pl.program_id(0) == 0) & (s == 0))
    def _():
        row = lax.broadcasted_iota(jnp.int32, (CHUNK, CHUNK), 0)
        col = lax.broadcasted_iota(jnp.int32, (CHUNK, CHUNK), 1)
        keep = col <= row
        for j in range(N_PAIRS):
            wsm_scr[j] = jnp.concatenate(
                [jnp.where(keep, ws_ref[0, 2 * j + n], 0.0) for n in range(2)], axis=1)
        bias_t = bs_ref[0].T
        for hd in range(N_SGU_HEADS):
            bsf_scr[:, hd * SGU_HEAD_DIM:(hd + 1) * SGU_HEAD_DIM] = jnp.broadcast_to(
                bias_t[:, hd:hd + 1], (CHUNK, SGU_HEAD_DIM))
        pieces = [(dst, src, r, min(TILE, dst.shape[0] - r), c)
                  for dst, src in ((w_in_scr, w_in_ref), (w_out_scr, w_out_ref),
                                   (w_ple_scr, w_ple_ref), (w_gate_scr, w_gate_ref))
                  for r in range(0, dst.shape[0], TILE)
                  for c in range(0, dst.shape[1], D_MODEL)]
        landing = (h_scr, h1_scr)

        def piece_copy(k):
            _, src, r, n, c = pieces[k]
            return pltpu.make_async_copy(src.at[0, r:r + n, c:c + D_MODEL],
                                         landing[k % 2].at[0:n, :], dma_sem.at[k % 2])

        piece_copy(0).start()
        piece_copy(1).start()
        for k, (dst, _, r, n, c) in enumerate(pieces):
            piece_copy(k).wait()
            for cc in range(0, D_MODEL, MXU_COLS):
                dst[r:r + n, c + cc:c + cc + MXU_COLS] = landing[k % 2][0:n, cc:cc + MXU_COLS].astype(
                    jnp.bfloat16)
            if k + 2 < len(pieces):
                piece_copy(k + 2).start()

    @pl.when(s == 0)
    def _():
        a_scr[:, 0:HALO, :] = jnp.zeros((N_CONV_SLABS, HALO, LANES), f32)

    for r in range(0, TILE, ROWS):
        h_scr[r:r + ROWS, :] = _ln(x_ref[0, r:r + ROWS, :], ln_g_ref[...], ln_b_ref[...])

    def proj_cols(col0, width):
        return _mm(h_scr[...], w_in_scr[:, col0:col0 + width])

    def proj(g):
        return proj_cols(g * 512, 512)

    half_cols = D_CONV // 2
    slabs_per_half = N_CONV_SLABS // 2

    def conv_slab(c):
        cols = slice(c * LANES, (c + 1) * LANES)
        for r in range(0, TILE, CONV_ROWS):
            acc = jnp.broadcast_to(conv_b_ref[:, cols], (CONV_ROWS, LANES))
            for k in range(CONV_WIDTH):
                off = r + HALO - (CONV_WIDTH - 1) + k
                acc = acc + conv_w_ref[0, k:k + 1, cols] * a_scr[c, off:off + CONV_ROWS, :]
            conv_scr[r:r + CONV_ROWS, cols] = acc

    for n in range(2):
        val = proj_cols(n * half_cols, half_cols)
        gate = proj_cols(D_CONV + n * half_cols, half_cols)
        a = val * jax.nn.sigmoid(gate)
        for j in range(slabs_per_half):
            c = n * slabs_per_half + j
            a_scr[c, HALO:HALO + TILE, :] = a[:, j * LANES:(j + 1) * LANES]
        if n == 0:
            for j in range(slabs_per_half):
                conv_slab(j)
    za_scr[...] = _silu(proj(2))
    u_scr[...] = _gelu(proj(3))
    v_scr[...] = _ln(_gelu(proj(4)), sln_g_ref[...], sln_b_ref[...])
    zb_scr[...] = _silu(proj(5))

    lane = lax.broadcasted_iota(jnp.int32, (CHUNK, 2 * SGU_HEAD_DIM), 1)
    lo = lane < SGU_HEAD_DIM
    def block_diag(vp):
        zero = jnp.zeros_like(vp)
        return jnp.concatenate([jnp.where(lo, vp, zero), jnp.where(lo, zero, vp)], axis=0)

    for c0 in range(0, TILE, 2 * CHUNK):
        for j in range(N_PAIRS):
            cols = slice(j * 2 * SGU_HEAD_DIM, (j + 1) * 2 * SGU_HEAD_DIM)
            chunks = (c0, c0 + CHUNK)
            rhs = jnp.concatenate([block_diag(v_scr[c:c + CHUNK, cols]) for c in chunks], axis=1)
            sv2 = jnp.dot(wsm_scr[j], rhs, preferred_element_type=f32)
            for i, c in enumerate(chunks):
                sv = sv2[:, i * 2 * SGU_HEAD_DIM:(i + 1) * 2 * SGU_HEAD_DIM] + bsf_scr[:, cols]
                yb = u_scr[c:c + CHUNK, cols] * sv * zb_scr[c:c + CHUNK, cols]
                yb_scr[c:c + CHUNK, cols] = yb

    h1_scr[...] = ALPHA * h_scr[...] + _mm(yb_scr[...], w_out_scr[D_CONV:, :])

    for j in range(slabs_per_half):
        conv_slab(slabs_per_half + j)

    for r in range(0, TILE, ROWS):
        ya = (_silu(_ln(conv_scr[r:r + ROWS, :], cln_g_ref[...], cln_b_ref[...]))
              * za_scr[r:r + ROWS, :])
        ya_scr[r:r + ROWS, :] = ya
    a_scr[:, 0:HALO, :] = a_scr[:, TILE:TILE + HALO, :]

    mix_a = _mm(ya_scr[...], w_out_scr[0:D_CONV, :])
    h_scr[...] = _ln(h1_scr[...] + mix_a, pln_g_ref[...], pln_b_ref[...])

    pe = _mm(p_ref[0, 0], w_ple_scr[...])
    gate = jax.nn.sigmoid(_mm(h_scr[...], w_gate_scr[...]) + b_gate_ref[...])
    out_ref[0] = h_scr[...] + gate * pe


def kernel(x, p, ln_emb_g, ln_emb_b, w_in, conv_w, conv_b, conv_ln_g, conv_ln_b, sgu_ln_g, sgu_ln_b, w_s, b_s, w_out, post_ln_g, post_ln_b, w_ple, w_ple_gate, b_ple_gate):
    batch, seq, d_model = x.shape
    assert d_model == D_MODEL and seq % TILE == 0 and p.shape[0] == DEPTH
    f32 = jnp.float32
    bf16 = jnp.bfloat16

    streamed = (w_in, w_out, w_ple, w_ple_gate)

    def const(arr):
        if any(arr is w for w in streamed):
            return pl.BlockSpec(memory_space=pl.ANY)
        return pl.BlockSpec(arr.shape, lambda b, s: (0,) * arr.ndim, pipeline_mode=pl.Buffered(1))

    params = (ln_emb_g, ln_emb_b, w_in, conv_w, conv_b, conv_ln_g, conv_ln_b, sgu_ln_g, sgu_ln_b,
              w_s, b_s, w_out, post_ln_g, post_ln_b, w_ple, w_ple_gate, b_ple_gate)
    in_specs = [
        pl.BlockSpec((1, TILE, D_MODEL), lambda b, s: (b, s, 0)),
        pl.BlockSpec((1, 1, TILE, D_PLE), lambda b, s: (0, b, s, 0)),
        *[const(arr) for arr in params],
    ]
    scratch = [
        pltpu.VMEM((TILE, D_MODEL), f32),
        pltpu.VMEM((N_CONV_SLABS, HALO + TILE, LANES), f32),
        pltpu.VMEM((TILE, D_CONV), f32),
        pltpu.VMEM((TILE, D_CONV), f32),
        pltpu.VMEM((TILE, D_SGU), f32),
        pltpu.VMEM((TILE, D_SGU), f32),
        pltpu.VMEM((TILE, D_SGU), f32),
        pltpu.VMEM((TILE, D_CONV), f32),
        pltpu.VMEM((TILE, D_SGU), f32),
        pltpu.VMEM((TILE, D_MODEL), f32),
        pltpu.VMEM((N_PAIRS, CHUNK, 2 * CHUNK), f32),
        pltpu.VMEM((CHUNK, D_SGU), f32),
        pltpu.VMEM(w_in.shape[1:], bf16),
        pltpu.VMEM(w_out.shape[1:], bf16),
        pltpu.VMEM(w_ple.shape[1:], bf16),
        pltpu.VMEM(w_ple_gate.shape[1:], bf16),
        pltpu.SemaphoreType.DMA((2,)),
    ]
    return pl.pallas_call(
        _layer_kernel,
        grid=(batch, seq // TILE),
        in_specs=in_specs,
        out_specs=pl.BlockSpec((1, TILE, D_MODEL), lambda b, s: (b, s, 0)),
        out_shape=jax.ShapeDtypeStruct(x.shape, x.dtype),
        scratch_shapes=scratch,
        compiler_params=pltpu.CompilerParams(
            dimension_semantics=("arbitrary", "arbitrary"),
            vmem_limit_bytes=VMEM_LIMIT_BYTES),
        name="deepnorm_hybrid_layer",
    )(x, p, *params)
```

```python
import jax
import jax.numpy as jnp
from jax import lax
from jax.experimental import pallas as pl
from jax.experimental.pallas import tpu as pltpu

D_MODEL = 1024
D_PLE = 256
D_CONV = 512
D_SGU = 512
CONV_WIDTH = 31
N_SGU_HEADS = 8
SGU_HEAD_DIM = 64
CHUNK = 128
LN_EPS = 1e-5
DEPTH = 1
ALPHA = (2 * DEPTH) ** 0.25

LANES = 128
MXU_COLS = 256
TILE = 256
ROWS = 64
CONV_ROWS = 128
HALO = 32
N_PAIRS = N_SGU_HEADS // 2
N_CONV_SLABS = D_CONV // LANES
VMEM_LIMIT_BYTES = 60 * 1024 * 1024

_INV_SQRT2 = 0.7071067811865476


def _ln(x, g, b):
    mu = jnp.mean(x, axis=-1, keepdims=True)
    xc = x - mu
    var = jnp.mean(xc * xc, axis=-1, keepdims=True)
    return xc * lax.rsqrt(var + LN_EPS) * g + b


def _silu(x):
    return x * jax.nn.sigmoid(x)


def _gelu(x):
    return 0.5 * x * (1.0 + lax.erf(x * _INV_SQRT2))


def _mm(lhs_f32, w_bf16):
    return lax.dot_general(lhs_f32, w_bf16, (((1,), (0,)), ((), ())),
                           preferred_element_type=jnp.float32)


def _layer_kernel(x_ref, p_ref, ln_g_ref, ln_b_ref, w_in_ref, conv_w_ref, conv_b_ref,
                  cln_g_ref, cln_b_ref, sln_g_ref, sln_b_ref, ws_ref, bs_ref, w_out_ref,
                  pln_g_ref, pln_b_ref, w_ple_ref, w_gate_ref, b_gate_ref,
                  out_ref,
                  h_scr, a_scr, conv_scr, za_scr, u_scr, v_scr, zb_scr, ya_scr, yb_scr, h1_scr,
                  wsm_scr, bsf_scr, w_in_scr, w_out_scr, w_ple_scr, w_gate_scr):
    s = pl.program_id(1)
    f32 = jnp.float32

    @pl.when((pl.program_id(0) == 0) & (s == 0))
    def _():
        row = lax.broadcasted_iota(jnp.int32, (CHUNK, CHUNK), 0)
        col = lax.broadcasted_iota(jnp.int32, (CHUNK, CHUNK), 1)
        keep = col <= row
        for j in range(N_PAIRS):
            wsm_scr[j] = jnp.concatenate(
                [jnp.where(keep, ws_ref[0, 2 * j + n], 0.0) for n in range(2)], axis=1)
        bias_t = bs_ref[0].T
        for hd in range(N_SGU_HEADS):
            bsf_scr[:, hd * SGU_HEAD_DIM:(hd + 1) * SGU_HEAD_DIM] = jnp.broadcast_to(
                bias_t[:, hd:hd + 1], (CHUNK, SGU_HEAD_DIM))
        for dst, src in ((w_in_scr, w_in_ref), (w_out_scr, w_out_ref), (w_ple_scr, w_ple_ref),
                         (w_gate_scr, w_gate_ref)):
            for c in range(0, dst.shape[1], MXU_COLS):
                dst[:, c:c + MXU_COLS] = src[0, :, c:c + MXU_COLS].astype(jnp.bfloat16)

    @pl.when(s == 0)
    def _():
        a_scr[:, 0:HALO, :] = jnp.zeros((N_CONV_SLABS, HALO, LANES), f32)

    for r in range(0, TILE, ROWS):
        h_scr[r:r + ROWS, :] = _ln(x_ref[0, r:r + ROWS, :], ln_g_ref[...], ln_b_ref[...])

    def proj_cols(col0, width):
        return _mm(h_scr[...], w_in_scr[:, col0:col0 + width])

    def proj(g):
        return proj_cols(g * 512, 512)

    half_cols = D_CONV // 2
    slabs_per_half = N_CONV_SLABS // 2

    def conv_slab(c):
        cols = slice(c * LANES, (c + 1) * LANES)
        for r in range(0, TILE, CONV_ROWS):
            acc = jnp.broadcast_to(conv_b_ref[:, cols], (CONV_ROWS, LANES))
            for k in range(CONV_WIDTH):
                off = r + HALO - (CONV_WIDTH - 1) + k
                acc = acc + conv_w_ref[0, k:k + 1, cols] * a_scr[c, off:off + CONV_ROWS, :]
            conv_scr[r:r + CONV_ROWS, cols] = acc

    for n in range(2):
        val = proj_cols(n * half_cols, half_cols)
        gate = proj_cols(D_CONV + n * half_cols, half_cols)
        a = val * jax.nn.sigmoid(gate)
        for j in range(slabs_per_half):
            c = n * slabs_per_half + j
            a_scr[c, HALO:HALO + TILE, :] = a[:, j * LANES:(j + 1) * LANES]
        if n == 0:
            for j in range(slabs_per_half):
                conv_slab(j)
    za_scr[...] = _silu(proj(2))
    u_scr[...] = _gelu(proj(3))
    v_scr[...] = _ln(_gelu(proj(4)), sln_g_ref[...], sln_b_ref[...])
    zb_scr[...] = _silu(proj(5))

    lane = lax.broadcasted_iota(jnp.int32, (CHUNK, 2 * SGU_HEAD_DIM), 1)
    lo = lane < SGU_HEAD_DIM
    def block_diag(vp):
        zero = jnp.zeros_like(vp)
        return jnp.concatenate([jnp.where(lo, vp, zero), jnp.where(lo, zero, vp)], axis=0)

    for c0 in range(0, TILE, 2 * CHUNK):
        for j in range(N_PAIRS):
            cols = slice(j * 2 * SGU_HEAD_DIM, (j + 1) * 2 * SGU_HEAD_DIM)
            chunks = (c0, c0 + CHUNK)
            rhs = jnp.concatenate([block_diag(v_scr[c:c + CHUNK, cols]) for c in chunks], axis=1)
            sv2 = jnp.dot(wsm_scr[j], rhs, preferred_element_type=f32)
            for i, c in enumerate(chunks):
                sv = sv2[:, i * 2 * SGU_HEAD_DIM:(i + 1) * 2 * SGU_HEAD_DIM] + bsf_scr[:, cols]
                yb = u_scr[c:c + CHUNK, cols] * sv * zb_scr[c:c + CHUNK, cols]
                yb_scr[c:c + CHUNK, cols] = yb

    h1_scr[...] = ALPHA * h_scr[...] + _mm(yb_scr[...], w_out_scr[D_CONV:, :])

    for j in range(slabs_per_half):
        conv_slab(slabs_per_half + j)

    for r in range(0, TILE, ROWS):
        ya = (_silu(_ln(conv_scr[r:r + ROWS, :], cln_g_ref[...], cln_b_ref[...]))
              * za_scr[r:r + ROWS, :])
        ya_scr[r:r + ROWS, :] = ya
    a_scr[:, 0:HALO, :] = a_scr[:, TILE:TILE + HALO, :]

    mix_a = _mm(ya_scr[...], w_out_scr[0:D_CONV, :])
    h_scr[...] = _ln(h1_scr[...] + mix_a, pln_g_ref[...], pln_b_ref[...])

    pe = _mm(p_ref[0, 0], w_ple_scr[...])
    gate = jax.nn.sigmoid(_mm(h_scr[...], w_gate_scr[...]) + b_gate_ref[...])
    out_ref[0] = h_scr[...] + gate * pe


def kernel(x, p, ln_emb_g, ln_emb_b, w_in, conv_w, conv_b, conv_ln_g, conv_ln_b, sgu_ln_g, sgu_ln_b, w_s, b_s, w_out, post_ln_g, post_ln_b, w_ple, w_ple_gate, b_ple_gate):
    batch, seq, d_model = x.shape
    assert d_model == D_MODEL and seq % TILE == 0 and p.shape[0] == DEPTH
    f32 = jnp.float32
    bf16 = jnp.bfloat16

    def const(arr):
        return pl.BlockSpec(arr.shape, lambda b, s: (0,) * arr.ndim, pipeline_mode=pl.Buffered(1))

    params = (ln_emb_g, ln_emb_b, w_in, conv_w, conv_b, conv_ln_g, conv_ln_b, sgu_ln_g, sgu_ln_b,
              w_s, b_s, w_out, post_ln_g, post_ln_b, w_ple, w_ple_gate, b_ple_gate)
    in_specs = [
        pl.BlockSpec((1, TILE, D_MODEL), lambda b, s: (b, s, 0)),
        pl.BlockSpec((1, 1, TILE, D_PLE), lambda b, s: (0, b, s, 0)),
        *[const(arr) for arr in params],
    ]
    scratch = [
        pltpu.VMEM((TILE, D_MODEL), f32),
        pltpu.VMEM((N_CONV_SLABS, HALO + TILE, LANES), f32),
        pltpu.VMEM((TILE, D_CONV), f32),
        pltpu.VMEM((TILE, D_CONV), f32),
        pltpu.VMEM((TILE, D_SGU), f32),
        pltpu.VMEM((TILE, D_SGU), f32),
        pltpu.VMEM((TILE, D_SGU), f32),
        pltpu.VMEM((TILE, D_CONV), f32),
        pltpu.VMEM((TILE, D_SGU), f32),
        pltpu.VMEM((TILE, D_MODEL), f32),
        pltpu.VMEM((N_PAIRS, CHUNK, 2 * CHUNK), f32),
        pltpu.VMEM((CHUNK, D_SGU), f32),
        pltpu.VMEM(w_in.shape[1:], bf16),
        pltpu.VMEM(w_out.shape[1:], bf16),
        pltpu.VMEM(w_ple.shape[1:], bf16),
        pltpu.VMEM(w_ple_gate.shape[1:], bf16),
    ]
    return pl.pallas_call(
        _layer_kernel,
        grid=(batch, seq // TILE),
        in_specs=in_specs,
        out_specs=pl.BlockSpec((1, TILE, D_MODEL), lambda b, s: (b, s, 0)),
        out_shape=jax.ShapeDtypeStruct(x.shape, x.dtype),
        scratch_shapes=scratch,
        compiler_params=pltpu.CompilerParams(
            dimension_semantics=("arbitrary", "arbitrary"),
            vmem_limit_bytes=VMEM_LIMIT_BYTES),
        name="deepnorm_hybrid_layer",
    )(x, p, *params)
```

```python
import jax
import jax.numpy as jnp
from jax import lax
from jax.experimental import pallas as pl
from jax.experimental.pallas import tpu as pltpu

D_MODEL = 1024
D_PLE = 256
D_CONV = 512
D_SGU = 512
CONV_WIDTH = 31
N_SGU_HEADS = 8
SGU_HEAD_DIM = 64
CHUNK = 128
LN_EPS = 1e-5
DEPTH = 1
ALPHA = (2 * DEPTH) ** 0.25

LANES = 128
MXU_COLS = 256
TILE = 512
ROWS = 64
CONV_ROWS = 128
HALO = 32
N_PAIRS = N_SGU_HEADS // 2
N_CONV_SLABS = D_CONV // LANES
VMEM_LIMIT_BYTES = 60 * 1024 * 1024

_INV_SQRT2 = 0.7071067811865476


def _ln(x, g, b):
    mu = jnp.mean(x, axis=-1, keepdims=True)
    xc = x - mu
    var = jnp.mean(xc * xc, axis=-1, keepdims=True)
    return xc * lax.rsqrt(var + LN_EPS) * g + b


def _silu(x):
    return x * jax.nn.sigmoid(x)


def _gelu(x):
    return 0.5 * x * (1.0 + lax.erf(x * _INV_SQRT2))


def _mm(lhs_f32, w_bf16):
    return lax.dot_general(lhs_f32, w_bf16, (((1,), (0,)), ((), ())),
                           preferred_element_type=jnp.float32)


def _layer_kernel(x_ref, p_ref, w_in_ref, ws_ref, w_out_ref, w_ple_ref, w_gate_ref, conv_w_ref,
                  ln_g_ref, ln_b_ref, conv_b_ref, cln_g_ref, cln_b_ref, sln_g_ref, sln_b_ref,
                  bs_ref, pln_g_ref, pln_b_ref, b_gate_ref,
                  out_ref,
                  h_scr, a_scr, conv_scr, za_scr, u_scr, v_scr, zb_scr, ya_scr, yb_scr, h1_scr,
                  wsm_scr, bsf_scr, w_in_scr, w_out_scr, w_ple_scr, w_gate_scr):
    s = pl.program_id(1)
    f32 = jnp.float32

    @pl.when((pl.program_id(0) == 0) & (s == 0))
    def _():
        row = lax.broadcasted_iota(jnp.int32, (CHUNK, CHUNK), 0)
        col = lax.broadcasted_iota(jnp.int32, (CHUNK, CHUNK), 1)
        keep = col <= row
        for j in range(N_PAIRS):
            wsm_scr[j] = jnp.concatenate(
                [jnp.where(keep, ws_ref[0, 2 * j + n], 0.0) for n in range(2)], axis=1)
        bias_t = bs_ref[0].T
        for hd in range(N_SGU_HEADS):
            bsf_scr[:, hd * SGU_HEAD_DIM:(hd + 1) * SGU_HEAD_DIM] = jnp.broadcast_to(
                bias_t[:, hd:hd + 1], (CHUNK, SGU_HEAD_DIM))
        for dst, src in ((w_in_scr, w_in_ref), (w_out_scr, w_out_ref), (w_ple_scr, w_ple_ref),
                         (w_gate_scr, w_gate_ref)):
            for c in range(0, dst.shape[1], MXU_COLS):
                dst[:, c:c + MXU_COLS] = src[0, :, c:c + MXU_COLS].astype(jnp.bfloat16)

    @pl.when(s == 0)
    def _():
        a_scr[:, 0:HALO, :] = jnp.zeros((N_CONV_SLABS, HALO, LANES), f32)

    for r in range(0, TILE, ROWS):
        h_scr[r:r + ROWS, :] = _ln(x_ref[0, r:r + ROWS, :], ln_g_ref[...], ln_b_ref[...])

    def proj_cols(col0, width):
        return _mm(h_scr[...], w_in_scr[:, col0:col0 + width])

    def proj(g):
        return proj_cols(g * 512, 512)

    half_cols = D_CONV // 2
    slabs_per_half = N_CONV_SLABS // 2

    def conv_slab(c):
        cols = slice(c * LANES, (c + 1) * LANES)
        for r in range(0, TILE, CONV_ROWS):
            acc = jnp.broadcast_to(conv_b_ref[:, cols], (CONV_ROWS, LANES))
            for k in range(CONV_WIDTH):
                off = r + HALO - (CONV_WIDTH - 1) + k
                acc = acc + conv_w_ref[0, k:k + 1, cols] * a_scr[c, off:off + CONV_ROWS, :]
            conv_scr[r:r + CONV_ROWS, cols] = acc

    for n in range(2):
        val = proj_cols(n * half_cols, half_cols)
        gate = proj_cols(D_CONV + n * half_cols, half_cols)
        a = val * jax.nn.sigmoid(gate)
        for j in range(slabs_per_half):
            c = n * slabs_per_half + j
            a_scr[c, HALO:HALO + TILE, :] = a[:, j * LANES:(j + 1) * LANES]
        if n == 0:
            for j in range(slabs_per_half):
                conv_slab(j)
    za_scr[...] = _silu(proj(2))
    u_scr[...] = _gelu(proj(3))
    v_scr[...] = _ln(_gelu(proj(4)), sln_g_ref[...], sln_b_ref[...])
    zb_scr[...] = _silu(proj(5))

    lane = lax.broadcasted_iota(jnp.int32, (CHUNK, 2 * SGU_HEAD_DIM), 1)
    lo = lane < SGU_HEAD_DIM
    def block_diag(vp):
        zero = jnp.zeros_like(vp)
        return jnp.concatenate([jnp.where(lo, vp, zero), jnp.where(lo, zero, vp)], axis=0)

    for c0 in range(0, TILE, 2 * CHUNK):
        for j in range(N_PAIRS):
            cols = slice(j * 2 * SGU_HEAD_DIM, (j + 1) * 2 * SGU_HEAD_DIM)
            chunks = (c0, c0 + CHUNK)
            rhs = jnp.concatenate([block_diag(v_scr[c:c + CHUNK, cols]) for c in chunks], axis=1)
            sv2 = jnp.dot(wsm_scr[j], rhs, preferred_element_type=f32)
            for i, c in enumerate(chunks):
                sv = sv2[:, i * 2 * SGU_HEAD_DIM:(i + 1) * 2 * SGU_HEAD_DIM] + bsf_scr[:, cols]
                yb = u_scr[c:c + CHUNK, cols] * sv * zb_scr[c:c + CHUNK, cols]
                yb_scr[c:c + CHUNK, cols] = yb

    h1_scr[...] = ALPHA * h_scr[...] + _mm(yb_scr[...], w_out_scr[D_CONV:, :])

    for j in range(slabs_per_half):
        conv_slab(slabs_per_half + j)

    for r in range(0, TILE, ROWS):
        ya = (_silu(_ln(conv_scr[r:r + ROWS, :], cln_g_ref[...], cln_b_ref[...]))
              * za_scr[r:r + ROWS, :])
        ya_scr[r:r + ROWS, :] = ya
    a_scr[:, 0:HALO, :] = a_scr[:, TILE:TILE + HALO, :]

    mix_a = _mm(ya_scr[...], w_out_scr[0:D_CONV, :])
    h_scr[...] = _ln(h1_scr[...] + mix_a, pln_g_ref[...], pln_b_ref[...])

    pe = _mm(p_ref[0, 0], w_ple_scr[...])
    gate = jax.nn.sigmoid(_mm(h_scr[...], w_gate_scr[...]) + b_gate_ref[...])
    out_ref[0] = h_scr[...] + gate * pe


def kernel(x, p, ln_emb_g, ln_emb_b, w_in, conv_w, conv_b, conv_ln_g, conv_ln_b, sgu_ln_g, sgu_ln_b, w_s, b_s, w_out, post_ln_g, post_ln_b, w_ple, w_ple_gate, b_ple_gate):
    batch, seq, d_model = x.shape
    assert d_model == D_MODEL and seq % TILE == 0 and p.shape[0] == DEPTH
    f32 = jnp.float32
    bf16 = jnp.bfloat16

    def const(arr):
        return pl.BlockSpec(arr.shape, lambda b, s: (0,) * arr.ndim, pipeline_mode=pl.Buffered(1))

    params = (w_in, w_s, w_out, w_ple, w_ple_gate, conv_w,
              ln_emb_g, ln_emb_b, conv_b, conv_ln_g, conv_ln_b, sgu_ln_g, sgu_ln_b,
              b_s, post_ln_g, post_ln_b, b_ple_gate)
    in_specs = [
        pl.BlockSpec((1, TILE, D_MODEL), lambda b, s: (b, s, 0)),
        pl.BlockSpec((1, 1, TILE, D_PLE), lambda b, s: (0, b, s, 0)),
        *[const(arr) for arr in params],
    ]
    scratch = [
        pltpu.VMEM((TILE, D_MODEL), f32),
        pltpu.VMEM((N_CONV_SLABS, HALO + TILE, LANES), f32),
        pltpu.VMEM((TILE, D_CONV), f32),
        pltpu.VMEM((TILE, D_CONV), f32),
        pltpu.VMEM((TILE, D_SGU), f32),
        pltpu.VMEM((TILE, D_SGU), f32),
        pltpu.VMEM((TILE, D_SGU), f32),
        pltpu.VMEM((TILE, D_CONV), f32),
        pltpu.VMEM((TILE, D_SGU), f32),
        pltpu.VMEM((TILE, D_MODEL), f32),
        pltpu.VMEM((N_PAIRS, CHUNK, 2 * CHUNK), f32),
        pltpu.VMEM((CHUNK, D_SGU), f32),
        pltpu.VMEM(w_in.shape[1:], bf16),
        pltpu.VMEM(w_out.shape[1:], bf16),
        pltpu.VMEM(w_ple.shape[1:], bf16),
        pltpu.VMEM(w_ple_gate.shape[1:], bf16),
    ]
    return pl.pallas_call(
        _layer_kernel,
        grid=(batch, seq // TILE),
        in_specs=in_specs,
        out_specs=pl.BlockSpec((1, TILE, D_MODEL), lambda b, s: (b, s, 0)),
        out_shape=jax.ShapeDtypeStruct(x.shape, x.dtype),
        scratch_shapes=scratch,
        compiler_params=pltpu.CompilerParams(
            dimension_semantics=("arbitrary", "arbitrary"),
            vmem_limit_bytes=VMEM_LIMIT_BYTES),
        name="deepnorm_hybrid_layer",
    )(x, p, *params)
```
